```python
import jax, jax.numpy as jnp
from jax import lax
import numpy as np

D_MODEL = 1024
BATCH = 8
SEQ = 4096
DEPTH = 4

GRID_W = 64
CTX_LEN = 256
D_FF = 2816
N_MOD = 9
W_FOURIER = 256
FOURIER_GROUPS = 4
W_LRU = 512
LRU_HEADS = 8
W_SC = 256
LRU_CONV = 4
SC_CONV = 3
LRU_C = 8.0
N_BRANCH = 3
RMS_EPS = 1e-6
OFF_F = 0
OFF_LX = OFF_F + W_FOURIER
OFF_LG = OFF_LX + W_LRU
OFF_SB = OFF_LG + W_LRU
OFF_SC = OFF_SB + W_SC
OFF_SX = OFF_SC + W_SC
OFF_G = OFF_SX + W_SC
IN_COLS = OFF_G + N_BRANCH * D_MODEL

kernel_name = "hybrid_fourier_rglru_shortconv_macaron_dit"


def rms_norm(x, g):
    xf = x.astype(jnp.float32)
    y = xf * lax.rsqrt(jnp.mean(xf * xf, axis=-1, keepdims=True) + RMS_EPS)
    return (y * g.astype(jnp.float32)).astype(x.dtype)


def modulate(x, shift, scale):
    return x * (1 + scale) + shift


def swiglu(x, wg, wu, wd):
    return (jax.nn.silu(x @ wg) * (x @ wu)) @ wd


def dw_conv(x, w, pad_left):
    k = w.shape[0]
    length = x.shape[1]
    xp = jnp.pad(x, ((0, 0), (pad_left, k - 1 - pad_left), (0, 0)))
    return sum(xp[:, j:j + length] * w[j] for j in range(k))


def lru_conv(x, w, b):
    return dw_conv(x, w, LRU_CONV // 2) + b


def short_conv_seq(x, w):
    return dw_conv(x, w, SC_CONV // 2)


def short_conv_latent(x, w):
    b, length, ch = x.shape
    rows = length // GRID_W
    y = dw_conv(x.reshape(b * rows, GRID_W, ch), w, SC_CONV // 2)
    return y.reshape(b, length, ch)


def fourier_mix(xf):
    b, length, _ = xf.shape
    z = xf.astype(jnp.float32).reshape(b, length, FOURIER_GROUPS, W_FOURIER // FOURIER_GROUPS)
    z = jnp.fft.fft2(z, axes=(1, 3), norm="ortho").real
    return z.reshape(b, length, W_FOURIER).astype(xf.dtype)


def _combine(left, right):
    return (left[0] * right[0], right[0] * left[1] + right[1])


def rglru_scan(xc, wa, ba, wx, bx, lam, h0):
    b, length, w = xc.shape
    xc = xc.astype(jnp.float32)
    xh = xc.reshape(b, length, LRU_HEADS, w // LRU_HEADS)
    r = jax.nn.sigmoid(jnp.einsum("blhi,hij->blhj", xh, wa.astype(jnp.float32))
                       + ba.astype(jnp.float32)).reshape(b, length, w)
    i = jax.nn.sigmoid(jnp.einsum("blhi,hij->blhj", xh, wx.astype(jnp.float32))
                       + bx.astype(jnp.float32)).reshape(b, length, w)
    log_a = -LRU_C * r * jax.nn.softplus(-lam.astype(jnp.float32))
    a = jnp.exp(log_a)
    bterm = jnp.sqrt(-jnp.expm1(2.0 * log_a)) * (i * xc)
    a_cum, h = lax.associative_scan(_combine, (a, bterm), axis=1)
    if h0 is not None:
        h = h + a_cum * h0[:, None, :]
    return h, h[:, -1]


def rglru_bidir(xc, wa, ba, wx, bx, lam, h0_f, h0_b):
    hf, last_f = rglru_scan(xc, wa[0], ba[0], wx[0], bx[0], lam[0], h0_f)
    hb_rev, last_b = rglru_scan(jnp.flip(xc, 1), wa[1], ba[1], wx[1], bx[1], lam[1], h0_b)
    return hf, jnp.flip(hb_rev, 1), last_f, last_b


def merge_branches(p, y_lru, sc_conv_fn, sc_w, wp_f, wp_l, wp_c, w_out):
    y_f = fourier_mix(p[..., OFF_F:OFF_LX])
    y_l = y_lru.astype(p.dtype) * jax.nn.gelu(p[..., OFF_LG:OFF_SB])
    y_c = p[..., OFF_SB:OFF_SC] * sc_conv_fn(p[..., OFF_SC:OFF_SX] * p[..., OFF_SX:OFF_G], sc_w)
    g_f, g_l, g_c = jnp.split(jax.nn.sigmoid(p[..., OFF_G:]), N_BRANCH, axis=-1)
    m = g_f * (y_f @ wp_f) + g_l * (y_l @ wp_l) + g_c * (y_c @ wp_c)
    return m @ w_out


def setup_inputs(seed: int = 0) -> dict:
    key = jax.random.key(seed)
    ks = jax.random.split(key, 26)

    def nrm(k, shape, scale):
        return jax.random.normal(k, shape, jnp.float32) * scale

    u = jax.random.uniform(ks[19], (DEPTH, 2, W_LRU), jnp.float32, 0.9, 0.999)
    s = u ** (1.0 / LRU_C)
    lru_lambda = jnp.log(s) - jnp.log1p(-s)
    blk = W_LRU // LRU_HEADS
    return {
        "x": nrm(ks[0], (BATCH, SEQ, D_MODEL), 1.0),
        "c": nrm(ks[1], (BATCH, D_MODEL), 1.0),
        "ctx": nrm(ks[2], (BATCH, CTX_LEN, D_MODEL), 1.0),
        "c_ctx": nrm(ks[3], (D_MODEL,), 1.0),
        "w_ada": nrm(ks[4], (DEPTH, D_MODEL, N_MOD * D_MODEL), 0.5 * D_MODEL ** -0.5),
        "b_ada": nrm(ks[5], (DEPTH, N_MOD * D_MODEL), 0.01),
        "g_ffn1": 1.0 + nrm(ks[6], (DEPTH, D_MODEL), 0.01),
        "g_mix": 1.0 + nrm(ks[7], (DEPTH, D_MODEL), 0.01),
        "g_ffn2": 1.0 + nrm(ks[8], (DEPTH, D_MODEL), 0.01),
        "ffn_w_gate": nrm(ks[9], (DEPTH, 2, D_MODEL, D_FF), D_MODEL ** -0.5),
        "ffn_w_up": nrm(ks[10], (DEPTH, 2, D_MODEL, D_FF), D_MODEL ** -0.5),
        "ffn_w_down": nrm(ks[11], (DEPTH, 2, D_FF, D_MODEL), D_FF ** -0.5),
        "w_in": nrm(ks[12], (DEPTH, D_MODEL, IN_COLS), D_MODEL ** -0.5),
        "lru_conv_w": nrm(ks[13], (DEPTH, LRU_CONV, W_LRU), LRU_CONV ** -0.5),
        "lru_conv_b": nrm(ks[14], (DEPTH, W_LRU), 0.01),
        "lru_wa": nrm(ks[15], (DEPTH, 2, LRU_HEADS, blk, blk), blk ** -0.5),
        "lru_ba": nrm(ks[16], (DEPTH, 2, LRU_HEADS, blk), 0.01),
        "lru_wx": nrm(ks[17], (DEPTH, 2, LRU_HEADS, blk, blk), blk ** -0.5),
        "lru_bx": nrm(ks[18], (DEPTH, 2, LRU_HEADS, blk), 0.01),
        "lru_lambda": lru_lambda,
        "sc_conv_w": nrm(ks[20], (DEPTH, SC_CONV, W_SC), SC_CONV ** -0.5),
        "wp_fourier": nrm(ks[21], (DEPTH, W_FOURIER, D_MODEL), W_FOURIER ** -0.5),
        "wp_lru": nrm(ks[22], (DEPTH, W_LRU, D_MODEL), W_LRU ** -0.5),
        "wp_conv": nrm(ks[23], (DEPTH, W_SC, D_MODEL), W_SC ** -0.5),
        "w_out": nrm(ks[24], (DEPTH, D_MODEL, D_MODEL), D_MODEL ** -0.5),
        "g_final": 1.0 + nrm(ks[25], (D_MODEL,), 0.01),
    }


def reference(x, c, ctx, c_ctx, w_ada, b_ada, g_ffn1, g_mix, g_ffn2, ffn_w_gate, ffn_w_up,
              ffn_w_down, w_in, lru_conv_w, lru_conv_b, lru_wa, lru_ba, lru_wx, lru_bx,
              lru_lambda, sc_conv_w, wp_fourier, wp_lru, wp_conv, w_out, g_final):
    h = x
    hc = ctx
    s_lat = jax.nn.silu(c)
    s_ctx = jax.nn.silu(c_ctx)
    for l in range(DEPTH):
        last = l == DEPTH - 1
        m = jnp.split((s_lat @ w_ada[l] + b_ada[l])[:, None, :], N_MOD, axis=-1)
        mc = jnp.split(s_ctx @ w_ada[l] + b_ada[l], N_MOD, axis=-1)

        h = h + 0.5 * m[2] * swiglu(modulate(rms_norm(h, g_ffn1[l]), m[0], m[1]),
                                    ffn_w_gate[l, 0], ffn_w_up[l, 0], ffn_w_down[l, 0])
        hc = hc + 0.5 * mc[2] * swiglu(modulate(rms_norm(hc, g_ffn1[l]), mc[0], mc[1]),
                                       ffn_w_gate[l, 0], ffn_w_up[l, 0], ffn_w_down[l, 0])

        u = modulate(rms_norm(h, g_mix[l]), m[3], m[4])
        uc = modulate(rms_norm(hc, g_mix[l]), mc[3], mc[4])

        if last:
            pc_lru = uc @ w_in[l][:, OFF_LX:OFF_LG]
        else:
            pc = uc @ w_in[l]
            pc_lru = pc[..., OFF_LX:OFF_LG]
        xcc = lru_conv(pc_lru, lru_conv_w[l], lru_conv_b[l])
        cf, cb, cf_last, cb_last = rglru_bidir(xcc, lru_wa[l], lru_ba[l], lru_wx[l], lru_bx[l],
                                               lru_lambda[l], None, None)

        p = u @ w_in[l]
        xcl = lru_conv(p[..., OFF_LX:OFF_LG], lru_conv_w[l], lru_conv_b[l])
        lf, lb, _, _ = rglru_bidir(xcl, lru_wa[l], lru_ba[l], lru_wx[l], lru_bx[l],
                                   lru_lambda[l], cf_last, cb_last)
        h = h + m[5] * merge_branches(p, lf + lb, short_conv_latent, sc_conv_w[l],
                                      wp_fourier[l], wp_lru[l], wp_conv[l], w_out[l])
        if not last:
            hc = hc + mc[5] * merge_branches(pc, cf + cb, short_conv_seq, sc_conv_w[l],
                                             wp_fourier[l], wp_lru[l], wp_conv[l], w_out[l])

        h = h + 0.5 * m[8] * swiglu(modulate(rms_norm(h, g_ffn2[l]), m[6], m[7]),
                                    ffn_w_gate[l, 1], ffn_w_up[l, 1], ffn_w_down[l, 1])
        if not last:
            hc = hc + 0.5 * mc[8] * swiglu(modulate(rms_norm(hc, g_ffn2[l]), mc[6], mc[7]),
                                           ffn_w_gate[l, 1], ffn_w_up[l, 1], ffn_w_down[l, 1])
    return rms_norm(h, g_final)
```

```python
import functools

import numpy as np
import jax
import jax.numpy as jnp
from jax import lax
from jax.experimental import pallas as pl
from jax.experimental.pallas import tpu as pltpu

GRID_W = 64
N_MOD = 9
W_FOURIER = 256
FOURIER_GROUPS = 4
W_LRU = 512
LRU_HEADS = 8
W_SC = 256
LRU_CONV = 4
SC_CONV = 3
LRU_C = 8.0
N_BRANCH = 3
RMS_EPS = 1e-6
OFF_F = 0
OFF_LX = OFF_F + W_FOURIER
OFF_LG = OFF_LX + W_LRU
OFF_SB = OFF_LG + W_LRU
OFF_SC = OFF_SB + W_SC
OFF_SX = OFF_SC + W_SC
OFF_G = OFF_SX + W_SC

LANES = 128
SUBLANES = 8
MXU_DIM = 256
VMEM_LIMIT = 56 * 1024 * 1024
MOD_ROWS = 16
LRU_CW = LANES
BF16 = jnp.bfloat16
F32 = jnp.float32


def _params(sem):
    return pltpu.CompilerParams(dimension_semantics=sem, vmem_limit_bytes=VMEM_LIMIT)


def _resident(block_shape, index_map):
    return pl.BlockSpec(block_shape, index_map, pipeline_mode=pl.Buffered(1))


def _dot(a, b):
    return jnp.dot(a, b, preferred_element_type=F32)


def _rms_mod(x, gain, shift, scale):
    y = x * lax.rsqrt(jnp.mean(x * x, axis=-1, keepdims=True) + RMS_EPS)
    return (y * gain) * (1.0 + scale) + shift


def _mod_kernel(s_ref, w_ref, b_ref, o_ref):
    s = jax.nn.silu(s_ref[...]).astype(BF16)
    o_ref[...] = _dot(s, w_ref[...].astype(BF16)) + b_ref[...]


def _modulation(c, c_ctx, w_ada, b_ada):
    depth, d, n = w_ada.shape
    batch = c.shape[0]
    rows = jnp.zeros((MOD_ROWS, d), F32).at[:batch].set(c).at[batch].set(c_ctx)
    tn = n // 8
    out = pl.pallas_call(
        _mod_kernel,
        out_shape=jax.ShapeDtypeStruct((depth, MOD_ROWS, n), F32),
        grid=(depth, n // tn),
        in_specs=[
            pl.BlockSpec((MOD_ROWS, d), lambda l, j: (0, 0)),
            pl.BlockSpec((None, d, tn), lambda l, j: (l, 0, j)),
            pl.BlockSpec((None, 1, tn), lambda l, j: (l, 0, j)),
        ],
        out_specs=pl.BlockSpec((None, MOD_ROWS, tn), lambda l, j: (l, 0, j)),
        compiler_params=_params(("arbitrary", "arbitrary")),
        name="modulation",
    )(rows, w_ada, b_ada.reshape(depth, 1, n))
    return out.reshape(depth, MOD_ROWS, N_MOD, d)


def _ffn_kernel(x_ref, mod_ref, g_ref, gfin_ref, wg_ref, wu_ref, wd_ref, o_ref, acc_ref,
                *, base, chunk, final):
    x = x_ref[...]
    u = _rms_mod(x, g_ref[...], mod_ref[base:base + 1, :], mod_ref[base + 1:base + 2, :]).astype(BF16)
    d_ff = wg_ref.shape[1]
    for j in range(d_ff // chunk):
        sl = slice(j * chunk, (j + 1) * chunk)
        a = (jax.nn.silu(_dot(u, wg_ref[:, sl])) * _dot(u, wu_ref[:, sl])).astype(BF16)
        part = _dot(a, wd_ref[sl, :])
        if j == 0:
            acc_ref[...] = part
        else:
            acc_ref[...] += part
    y = x + (0.5 * mod_ref[base + 2:base + 3, :]) * acc_ref[...]
    if final:
        y = (y * lax.rsqrt(jnp.mean(y * y, axis=-1, keepdims=True) + RMS_EPS)) * gfin_ref[...]
    o_ref[...] = y


def _ffn(h, mod, gain, g_final, wg, wu, wd, *, layer, half, mod_row, base, tm, final=False):
    nb, t, d = h.shape
    d_ff = wg.shape[-1]
    row = (lambda b: b) if mod_row is None else (lambda b: mod_row)
    return pl.pallas_call(
        functools.partial(_ffn_kernel, base=base, chunk=MXU_DIM, final=final),
        out_shape=jax.ShapeDtypeStruct(h.shape, F32),
        grid=(nb, t // tm),
        in_specs=[
            pl.BlockSpec((None, tm, d), lambda b, i: (b, i, 0)),
            pl.BlockSpec((None, None, N_MOD, d), lambda b, i: (layer, row(b), 0, 0)),
            _resident((None, 1, d), lambda b, i: (layer, 0, 0)),
            _resident((1, d), lambda b, i: (0, 0)),
            _resident((None, None, d, d_ff), lambda b, i: (layer, half, 0, 0)),
            _resident((None, None, d, d_ff), lambda b, i: (layer, half, 0, 0)),
            _resident((None, None, d_ff, d), lambda b, i: (layer, half, 0, 0)),
        ],
        out_specs=pl.BlockSpec((None, tm, d), lambda b, i: (b, i, 0)),
        scratch_shapes=[pltpu.VMEM((tm, d), F32)],
        compiler_params=_params(("arbitrary", "arbitrary")),
        name="ffn",
    )(h, mod, gain, g_final, wg, wu, wd)


def _inproj_kernel(x_ref, mod_ref, g_ref, w_ref, cdft_ref, scw_ref,
                   z_ref, lx_ref, gl_ref, yc_ref, gate_ref, *, row_len):
    tm = x_ref.shape[0]
    u = _rms_mod(x_ref[...], g_ref[...], mod_ref[3:4, :], mod_ref[4:5, :]).astype(BF16)

    p_f = _dot(u, w_ref[:, OFF_F:OFF_LX]).astype(BF16)
    z = _dot(p_f, cdft_ref[...])
    z_ref[0] = z[:, :W_FOURIER].astype(BF16)
    z_ref[1] = z[:, W_FOURIER:].astype(BF16)

    lx_ref[...] = _dot(u, w_ref[:, OFF_LX:OFF_LG])
    gl_ref[...] = jax.nn.gelu(_dot(u, w_ref[:, OFF_LG:OFF_SB])).astype(BF16)

    sc = _dot(u, w_ref[:, OFF_SB:OFF_G])
    prod = sc[:, W_SC:2 * W_SC] * sc[:, 2 * W_SC:]
    pos = lax.broadcasted_iota(jnp.int32, (tm, 1), 0) % row_len
    left = jnp.where(pos == 0, 0.0, pltpu.roll(prod, 1, axis=0))
    right = jnp.where(pos == row_len - 1, 0.0, pltpu.roll(prod, tm - 1, axis=0))
    conv = scw_ref[0:1, :] * left + scw_ref[1:2, :] * prod + scw_ref[2:3, :] * right
    yc_ref[...] = (sc[:, :W_SC] * conv).astype(BF16)

    n_gate = gate_ref.shape[1]
    chunk = 2 * MXU_DIM
    for j in range(n_gate // chunk):
        sl = slice(j * chunk, (j + 1) * chunk)
        gate_ref[:, sl] = jax.nn.sigmoid(_dot(u, w_ref[:, OFF_G + j * chunk:OFF_G + (j + 1) * chunk])).astype(BF16)


def _inproj(h, mod, gain, w_in, cdft, sc_w, *, layer, mod_row, tm, row_len):
    nb, t, d = h.shape
    n_cols = w_in.shape[-1]
    n_gate = n_cols - OFF_G
    assert tm % row_len == 0 and t % tm == 0
    row = (lambda b: b) if mod_row is None else (lambda b: mod_row)
    tok = lambda w: pl.BlockSpec((None, tm, w), lambda b, i: (b, i, 0))
    return pl.pallas_call(
        functools.partial(_inproj_kernel, row_len=row_len),
        out_shape=(
            jax.ShapeDtypeStruct((2, t, nb * W_FOURIER), BF16),
            jax.ShapeDtypeStruct((nb, t, W_LRU), F32),
            jax.ShapeDtypeStruct((nb, t, W_LRU), BF16),
            jax.ShapeDtypeStruct((nb, t, W_SC), BF16),
            jax.ShapeDtypeStruct((nb, t, n_gate), BF16),
        ),
        grid=(nb, t // tm),
        in_specs=[
            tok(d),
            pl.BlockSpec((None, None, N_MOD, d), lambda b, i: (layer, row(b), 0, 0)),
            _resident((None, 1, d), lambda b, i: (layer, 0, 0)),
            _resident((None, d, n_cols), lambda b, i: (layer, 0, 0)),
            _resident(cdft.shape, lambda b, i: (0, 0)),
            _resident((None, SC_CONV, W_SC), lambda b, i: (layer, 0, 0)),
        ],
        out_specs=(
            pl.BlockSpec((2, tm, W_FOURIER), lambda b, i: (0, i, b)),
            tok(W_LRU), tok(W_LRU), tok(W_SC), tok(n_gate),
        ),
        compiler_params=_params(("arbitrary", "arbitrary")),
        name="inproj",
    )(h, mod, gain, w_in, cdft, sc_w)


def _scan_layout(t):
    n_seg = 32 if t >= 1024 else 8
    seg_len = -(-t // n_seg)
    while seg_len % 8 != 4:
        seg_len += 1
    return n_seg, seg_len


def _lru_kernel(lx_ref, gl_ref, h0_ref, cw_ref, cb_ref, w_ref, b_ref, lam_ref, y_ref, last_ref,
                xpad, a_f, b_f, a_b, b_b, h_f, h_b, hin_f, hin_b, *, n_seg, seg_len, chunk):
    t, cw = lx_ref.shape
    t_pad = n_seg * seg_len
    halo = SUBLANES

    xpad[0:halo, :] = jnp.zeros((halo, cw), F32)
    xpad[halo + t:2 * halo + t, :] = jnp.zeros((halo, cw), F32)
    xpad[halo:halo + t, :] = lx_ref[...]
    if t_pad > t:
        for ref, val in ((a_f, 1.0), (a_b, 1.0), (b_f, 0.0), (b_b, 0.0)):
            ref[t:t_pad, :] = jnp.full((t_pad - t, cw), val, F32)

    neg_lam = -lam_ref[...]
    decay = -LRU_C * (jnp.maximum(neg_lam, 0.0) + jnp.log1p(jnp.exp(-jnp.abs(neg_lam))))

    for k in range(t // chunk):
        first = k * chunk + halo - LRU_CONV // 2
        xc = cb_ref[...]
        for j in range(LRU_CONV):
            xc = xc + cw_ref[j:j + 1, :] * xpad[first + j:first + j + chunk, :]
        gates = _dot(xc.astype(BF16), w_ref[...]) + b_ref[...]
        rows = slice(k * chunk, (k + 1) * chunk)
        for d, (a_ref, b_ref_d) in enumerate(((a_f, b_f), (a_b, b_b))):
            r = jax.nn.sigmoid(gates[:, (2 * d) * cw:(2 * d + 1) * cw])
            i = jax.nn.sigmoid(gates[:, (2 * d + 1) * cw:(2 * d + 2) * cw])
            log_a = decay[d:d + 1, :] * r
            a = jnp.exp(log_a)
            one_minus_a2 = -jnp.tanh(log_a) * (a * a + 1.0)
            a_ref[rows, :] = a
            b_ref_d[rows, :] = jnp.sqrt(one_minus_a2) * (i * xc)

    def seg(ref, j):
        return ref[pl.ds(j, n_seg, stride=seg_len), :]

    def summarize(j, carry):
        e_f, p_f, e_b, p_b = carry
        jb = seg_len - 1 - j
        af, ab = seg(a_f, j), seg(a_b, jb)
        return af * e_f + seg(b_f, j), p_f * af, ab * e_b + seg(b_b, jb), p_b * ab

    zeros = jnp.zeros((n_seg, cw), F32)
    ones = jnp.ones((n_seg, cw), F32)
    e_f, p_f, e_b, p_b = lax.fori_loop(0, seg_len, summarize, (zeros, ones, zeros, ones))

    state = h0_ref[0:1, :]
    for s in range(n_seg):
        hin_f[s:s + 1, :] = state
        state = p_f[s:s + 1, :] * state + e_f[s:s + 1, :]
    last_ref[0:1, :] = state
    state = h0_ref[1:2, :]
    for s in reversed(range(n_seg)):
        hin_b[s:s + 1, :] = state
        state = p_b[s:s + 1, :] * state + e_b[s:s + 1, :]
    last_ref[1:2, :] = state

    def emit(j, carry):
        s_f, s_b = carry
        jb = seg_len - 1 - j
        s_f = seg(a_f, j) * s_f + seg(b_f, j)
        s_b = seg(a_b, jb) * s_b + seg(b_b, jb)
        h_f[pl.ds(j, n_seg, stride=seg_len), :] = s_f
        h_b[pl.ds(jb, n_seg, stride=seg_len), :] = s_b
        return s_f, s_b

    lax.fori_loop(0, seg_len, emit, (hin_f[...], hin_b[...]))
    y_ref[...] = ((h_f[0:t, :] + h_b[0:t, :]) * gl_ref[...].astype(F32)).astype(BF16)


def _lru(lx, gl, h0, conv_w, conv_b, w_cat, b_cat, lam, *, layer):
    nb, t, w = lx.shape
    cw = LRU_CW
    n_seg, seg_len = _scan_layout(t)
    t_pad = n_seg * seg_len
    chunk = min(t, 512)
    seq = lambda: pl.BlockSpec((None, t, cw), lambda b, c: (b, 0, c))
    state = lambda: pl.BlockSpec((None, 2, cw), lambda b, c: (b, 0, c))
    scan_buf = pltpu.VMEM((t_pad, cw), F32)
    return pl.pallas_call(
        functools.partial(_lru_kernel, n_seg=n_seg, seg_len=seg_len, chunk=chunk),
        out_shape=(jax.ShapeDtypeStruct((nb, t, w), BF16), jax.ShapeDtypeStruct((nb, 2, w), F32)),
        grid=(nb, w // cw),
        in_specs=[
            seq(), seq(), state(),
            pl.BlockSpec((None, LRU_CONV, cw), lambda b, c: (layer, 0, c)),
            pl.BlockSpec((None, 1, cw), lambda b, c: (layer, 0, c)),
            pl.BlockSpec((None, None, cw, 4 * cw), lambda b, c: (layer, c, 0, 0)),
            pl.BlockSpec((None, None, 1, 4 * cw), lambda b, c: (layer, c, 0, 0)),
            pl.BlockSpec((None, 2, cw), lambda b, c: (layer, 0, c)),
        ],
        out_specs=(seq(), state()),
        scratch_shapes=[pltpu.VMEM((t + 2 * SUBLANES, cw), F32)] + [scan_buf] * 6
        + [pltpu.VMEM((n_seg, cw), F32)] * 2,
        compiler_params=_params(("arbitrary", "arbitrary")),
        name="lru",
    )(lx, gl, h0, conv_w, conv_b, w_cat, b_cat, lam)


def _lru_gate_weights(wa, ba, wx, bx):
    depth, _, heads, blk, _ = wa.shape
    per = LRU_CW // blk
    groups = heads // per
    eye = jnp.eye(per, dtype=wa.dtype)

    def block_diag(w):
        w = w.reshape(depth, 2, groups, per, blk, blk)
        return jnp.einsum("ldgaij,ab->ldgaibj", w, eye).reshape(depth, 2, groups, LRU_CW, LRU_CW)

    w = jnp.stack([block_diag(wa), block_diag(wx)], axis=2)
    w = w.transpose(0, 3, 4, 1, 2, 5).reshape(depth, groups, LRU_CW, 4 * LRU_CW)
    b = jnp.stack([ba.reshape(depth, 2, groups, LRU_CW), bx.reshape(depth, 2, groups, LRU_CW)], axis=2)
    b = b.transpose(0, 3, 1, 2, 4).reshape(depth, groups, 1, 4 * LRU_CW)
    return w.astype(BF16), b


def _seqdft_kernel(d_ref, z_ref, o_ref, acc_ref):
    k = pl.program_id(1)
    part = _dot(d_ref[...], z_ref[...])

    @pl.when(k == 0)
    def _():
        acc_ref[...] = part

    @pl.when(k > 0)
    def _():
        acc_ref[...] += part

    @pl.when(k == pl.num_programs(1) - 1)
    def _():
        o_ref[...] = acc_ref[...].astype(BF16)


def _seqdft(dmat, z):
    t, k2 = dmat.shape
    n = z.shape[-1]
    z2 = z.reshape(k2, n)
    tm = min(t, 1024)
    tk = min(k2, 1024)
    return pl.pallas_call(
        _seqdft_kernel,
        out_shape=jax.ShapeDtypeStruct((t, n), BF16),
        grid=(t // tm, k2 // tk),
        in_specs=[
            pl.BlockSpec((tm, tk), lambda i, k: (i, k)),
            pl.BlockSpec((tk, n), lambda i, k: (k, 0)),
        ],
        out_specs=pl.BlockSpec((tm, n), lambda i, k: (i, 0)),
        scratch_shapes=[pltpu.VMEM((tm, n), F32)],
        compiler_params=_params(("arbitrary", "arbitrary")),
        name="seqdft",
    )(dmat, z2)


def _channel_dft():
    gw = W_FOURIER // FOURIER_GROUPS
    idx = np.arange(gw)
    ang = 2.0 * np.pi * ((idx[:, None] * idx[None, :]) % gw) / gw
    eye = np.eye(FOURIER_GROUPS)
    cos = np.kron(eye, np.cos(ang)) / np.sqrt(gw)
    sin = np.kron(eye, np.sin(ang)) / np.sqrt(gw)
    return jnp.asarray(np.concatenate([cos, -sin], axis=1), dtype=BF16)


def _sequence_dft(t):
    def table(n_cols, col_step):
        j = lax.broadcasted_iota(jnp.int32, (t, n_cols), 0)
        k = lax.broadcasted_iota(jnp.int32, (t, n_cols), 1) * col_step
        ang = ((j * k) % t).astype(F32) * (2.0 * np.pi / t)
        return jnp.cos(ang), jnp.sin(ang)

    q = 64 if t % 64 == 0 and t > 64 else 1
    c1, s1 = table(t // q, q)
    c2, s2 = table(q, 1)
    cos = c1[:, :, None] * c2[:, None, :] - s1[:, :, None] * s2[:, None, :]
    sin = s1[:, :, None] * c2[:, None, :] + c1[:, :, None] * s2[:, None, :]
    both = jnp.concatenate([cos.reshape(t, t), sin.reshape(t, t)], axis=1)
    return (both * (1.0 / np.sqrt(t))).astype(BF16)


def _merge_kernel(h_ref, mod_ref, yf_ref, yl_ref, yc_ref, gate_ref, wpf_ref, wpl_ref, wpc_ref, wo_ref, o_ref):
    d = h_ref.shape[1]
    m = gate_ref[:, 0:d].astype(F32) * _dot(yf_ref[...], wpf_ref[...])
    m = m + gate_ref[:, d:2 * d].astype(F32) * _dot(yl_ref[...], wpl_ref[...])
    m = m + gate_ref[:, 2 * d:3 * d].astype(F32) * _dot(yc_ref[...], wpc_ref[...])
    o_ref[...] = h_ref[...] + mod_ref[5:6, :] * _dot(m.astype(BF16), wo_ref[...])


def _merge(h, mod, yf, yl, yc, gate, wpf, wpl, wpc, wo, *, layer, mod_row, tm):
    nb, t, d = h.shape
    row = (lambda b: b) if mod_row is None else (lambda b: mod_row)
    tok = lambda w: pl.BlockSpec((None, tm, w), lambda b, i: (b, i, 0))
    wspec = lambda w: _resident((None, w, d), lambda b, i: (layer, 0, 0))
    return pl.pallas_call(
        _merge_kernel,
        out_shape=jax.ShapeDtypeStruct(h.shape, F32),
        grid=(nb, t // tm),
        in_specs=[
            tok(d),
            pl.BlockSpec((None, None, N_MOD, d), lambda b, i: (layer, row(b), 0, 0)),
            pl.BlockSpec((tm, W_FOURIER), lambda b, i: (i, b)),
            tok(W_LRU), tok(W_SC), tok(N_BRANCH * d),
            wspec(W_FOURIER), wspec(W_LRU), wspec(W_SC), wspec(d),
        ],
        out_specs=tok(d),
        compiler_params=_params(("arbitrary", "arbitrary")),
        name="merge",
    )(h, mod, yf, yl, yc, gate, wpf, wpl, wpc, wo)


def kernel(x, c, ctx, c_ctx, w_ada, b_ada, g_ffn1, g_mix, g_ffn2, ffn_w_gate, ffn_w_up, ffn_w_down, w_in, lru_conv_w, lru_conv_b, lru_wa, lru_ba, lru_wx, lru_bx, lru_lambda, sc_conv_w, wp_fourier, wp_lru, wp_conv, w_out, g_final):
    depth, d = g_ffn1.shape
    batch, seq, _ = x.shape
    ctx_len = ctx.shape[1]
    ctx_row = batch

    mod = _modulation(c, c_ctx, w_ada, b_ada)
    wg, wu, wd = ffn_w_gate.astype(BF16), ffn_w_up.astype(BF16), ffn_w_down.astype(BF16)
    w_in_b = w_in.astype(BF16)
    wpf, wpl, wpc, wo = (w.astype(BF16) for w in (wp_fourier, wp_lru, wp_conv, w_out))
    gate_w, gate_b = _lru_gate_weights(lru_wa, lru_ba, lru_wx, lru_bx)
    gains = [g.reshape(depth, 1, d) for g in (g_ffn1, g_mix, g_ffn2)]
    g_fin = g_final.reshape(1, d)
    conv_b = lru_conv_b.reshape(depth, 1, W_LRU)
    cdft = _channel_dft()
    dft_lat, dft_ctx = _sequence_dft(seq), _sequence_dft(ctx_len)
    zero_state = jnp.zeros((batch, 2, W_LRU), F32)

    lat = dict(mod_row=None, tm=512)
    cx = dict(mod_row=ctx_row, tm=ctx_len)

    h, hc = x, ctx
    for l in range(depth):
        last = l == depth - 1
        ffn = functools.partial(_ffn, mod=mod, g_final=g_fin, wg=wg, wu=wu, wd=wd, layer=l)
        inproj = functools.partial(_inproj, mod=mod, gain=gains[1], w_in=w_in_b, cdft=cdft, sc_w=sc_conv_w, layer=l)
        lru = functools.partial(_lru, conv_w=lru_conv_w, conv_b=conv_b, w_cat=gate_w, b_cat=gate_b,
                                lam=lru_lambda, layer=l)
        merge = functools.partial(_merge, mod=mod, wpf=wpf, wpl=wpl, wpc=wpc, wo=wo, layer=l)

        hc = ffn(hc, gain=gains[0], half=0, base=0, **cx)
        h = ffn(h, gain=gains[0], half=0, base=0, **lat)

        zc, lxc, glc, ycc, gatec = inproj(hc, row_len=ctx_len, **cx)
        ylc, ctx_state = lru(lxc, glc, zero_state)
        z, lx, gl, yc, gate = inproj(h, row_len=GRID_W, **lat)
        yl, _ = lru(lx, gl, ctx_state)
        h = merge(h, yf=_seqdft(dft_lat, z), yl=yl, yc=yc, gate=gate, **lat)
        h = ffn(h, gain=gains[2], half=1, base=6, final=last, **lat)
        if not last:
            hc = merge(hc, yf=_seqdft(dft_ctx, zc), yl=ylc, yc=ycc, gate=gatec, **cx)
            hc = ffn(hc, gain=gains[2], half=1, base=6, **cx)
    return h
```

```python
import functools

import numpy as np
import jax
import jax.numpy as jnp
from jax import lax
from jax.experimental import pallas as pl
from jax.experimental.pallas import tpu as pltpu

GRID_W = 64
N_MOD = 9
W_FOURIER = 256
FOURIER_GROUPS = 4
W_LRU = 512
LRU_HEADS = 8
W_SC = 256
LRU_CONV = 4
SC_CONV = 3
LRU_C = 8.0
N_BRANCH = 3
RMS_EPS = 1e-6
OFF_F = 0
OFF_LX = OFF_F + W_FOURIER
OFF_LG = OFF_LX + W_LRU
OFF_SB = OFF_LG + W_LRU
OFF_SC = OFF_SB + W_SC
OFF_SX = OFF_SC + W_SC
OFF_G = OFF_SX + W_SC
PERM_F = 0
PERM_SB = PERM_F + W_FOURIER
PERM_SC = PERM_SB + W_SC
PERM_SX = PERM_SC + W_SC
PERM_LX = PERM_SX + W_SC
PERM_LG = PERM_LX + W_LRU
PERM_G = PERM_LG + W_LRU

LANES = 128
SUBLANES = 8
MXU_DIM = 256
VMEM_LIMIT = 56 * 1024 * 1024
MOD_ROWS = 16
LRU_CW = LANES
BF16 = jnp.bfloat16
F32 = jnp.float32
F32_TINY = float(np.finfo(np.float32).tiny)
SCAN_UNROLL = 4


def _params(sem):
    return pltpu.CompilerParams(dimension_semantics=sem, vmem_limit_bytes=VMEM_LIMIT)


def _resident(block_shape, index_map):
    return pl.BlockSpec(block_shape, index_map, pipeline_mode=pl.Buffered(1))


def _dot(a, b):
    return jnp.dot(a, b, preferred_element_type=F32)


def _rms_mod(x, gain, shift, scale):
    y = x * lax.rsqrt(jnp.mean(x * x, axis=-1, keepdims=True) + RMS_EPS)
    return (y * gain) * (1.0 + scale) + shift


def _mod_kernel(s_ref, w_ref, b_ref, o_ref):
    s = jax.nn.silu(s_ref[...]).astype(BF16)
    part = _dot(s, w_ref[...].astype(BF16))

    @pl.when(pl.program_id(1) == 0)
    def _():
        o_ref[...] = part + b_ref[...]

    @pl.when(pl.program_id(1) > 0)
    def _():
        o_ref[...] += part


def _modulation(c, c_ctx, w_ada, b_ada):
    depth, d, n = w_ada.shape
    batch = c.shape[0]
    rows = jnp.zeros((MOD_ROWS, d), F32).at[:batch].set(c).at[batch].set(c_ctx)
    tk = MXU_DIM
    out = pl.pallas_call(
        _mod_kernel,
        out_shape=jax.ShapeDtypeStruct((depth, MOD_ROWS, n), F32),
        grid=(depth, d // tk),
        in_specs=[
            pl.BlockSpec((MOD_ROWS, tk), lambda l, k: (0, k)),
            pl.BlockSpec((None, tk, n), lambda l, k: (l, k, 0)),
            pl.BlockSpec((None, 1, n), lambda l, k: (l, 0, 0)),
        ],
        out_specs=pl.BlockSpec((None, MOD_ROWS, n), lambda l, k: (l, 0, 0)),
        compiler_params=_params(("arbitrary", "arbitrary")),
        name="modulation",
    )(rows, w_ada, b_ada.reshape(depth, 1, n))
    return out.reshape(depth, MOD_ROWS, N_MOD, d)


def _swiglu_step(x, mod_ref, g_ref, gfin_ref, wg_ref, wu_ref, wd_ref, acc_ref, *, base, chunk, final):
    u = _rms_mod(x, g_ref[...], mod_ref[base:base + 1, :], mod_ref[base + 1:base + 2, :]).astype(BF16)
    d_ff = wg_ref.shape[1]
    for j in range(d_ff // chunk):
        sl = slice(j * chunk, (j + 1) * chunk)
        a = (jax.nn.silu(_dot(u, wg_ref[:, sl])) * _dot(u, wu_ref[:, sl])).astype(BF16)
        part = _dot(a, wd_ref[sl, :])
        if j == 0:
            acc_ref[...] = part
        else:
            acc_ref[...] += part
    y = x + (0.5 * mod_ref[base + 2:base + 3, :]) * acc_ref[...]
    if final:
        y = (y * lax.rsqrt(jnp.mean(y * y, axis=-1, keepdims=True) + RMS_EPS)) * gfin_ref[...]
    return y


def _ffn_kernel(x_ref, mod_ref, *rest, base, final):
    *ffn_refs, o_ref, acc_ref = rest
    o_ref[...] = _swiglu_step(x_ref[...], mod_ref, *ffn_refs, acc_ref, base=base, chunk=MXU_DIM, final=final)


def _mergeffn_kernel(h_ref, mod_ref, yf_ref, yl_ref, yc_ref, gate_ref, wpf_ref, wpl_ref, wpc_ref, wo_ref,
                     *rest, base, final):
    *ffn_refs, o_ref, acc_ref = rest
    d = h_ref.shape[1]
    m = gate_ref[:, 0:d].astype(F32) * _dot(yf_ref[...], wpf_ref[...])
    m = m + gate_ref[:, d:2 * d].astype(F32) * _dot(yl_ref[...], wpl_ref[...])
    m = m + gate_ref[:, 2 * d:3 * d].astype(F32) * _dot(yc_ref[...], wpc_ref[...])
    h = h_ref[...] + mod_ref[5:6, :] * _dot(m.astype(BF16), wo_ref[...])
    o_ref[...] = _swiglu_step(h, mod_ref, *ffn_refs, acc_ref, base=base, chunk=MXU_DIM, final=final)


def _ffn(h, mod, gain, g_final, wg, wu, wd, *, layer, half, mod_row, base, tm, final=False, merge=None):
    nb, t, d = h.shape
    d_ff = wg.shape[-1]
    row = (lambda b: b) if mod_row is None else (lambda b: mod_row)
    tok = lambda w: pl.BlockSpec((None, tm, w), lambda b, i: (b, i, 0))
    merge_specs, merge_args, body = [], (), _ffn_kernel
    if merge is not None:
        wspec = lambda w: _resident((None, w, d), lambda b, i: (layer, 0, 0))
        merge_specs = [pl.BlockSpec((tm, W_FOURIER), lambda b, i: (i, b)),
                       tok(W_LRU), tok(W_SC), tok(N_BRANCH * d),
                       wspec(W_FOURIER), wspec(W_LRU), wspec(W_SC), wspec(d)]
        merge_args, body = tuple(merge), _mergeffn_kernel
    return pl.pallas_call(
        functools.partial(body, base=base, final=final),
        out_shape=jax.ShapeDtypeStruct(h.shape, F32),
        grid=(nb, t // tm),
        in_specs=[
            tok(d),
            pl.BlockSpec((None, None, N_MOD, d), lambda b, i: (layer, row(b), 0, 0)),
            *merge_specs,
            _resident((None, 1, d), lambda b, i: (layer, 0, 0)),
            _resident((1, d), lambda b, i: (0, 0)),
            _resident((None, None, d, d_ff), lambda b, i: (layer, half, 0, 0)),
            _resident((None, None, d, d_ff), lambda b, i: (layer, half, 0, 0)),
            _resident((None, None, d_ff, d), lambda b, i: (layer, half, 0, 0)),
        ],
        out_specs=tok(d),
        scratch_shapes=[pltpu.VMEM((tm, d), F32)],
        compiler_params=_params(("arbitrary", "arbitrary")),
        name="ffn" if merge is None else "mergeffn",
    )(h, mod, *merge_args, gain, g_final, wg, wu, wd)


def _inproj_kernel(x_ref, mod_ref, g_ref, w_ref, cdft_ref, scw_ref,
                   z_ref, lx_ref, gl_ref, yc_ref, gate_ref, *, row_len):
    tm = x_ref.shape[0]
    u = _rms_mod(x_ref[...], g_ref[...], mod_ref[3:4, :], mod_ref[4:5, :]).astype(BF16)

    def proj(start):
        return _dot(u, w_ref[:, start:start + 2 * MXU_DIM])

    f_sb = proj(PERM_F)
    z = _dot(f_sb[:, :W_FOURIER].astype(BF16), cdft_ref[...])
    z_ref[0] = z[:, :W_FOURIER].astype(BF16)
    z_ref[1] = z[:, W_FOURIER:].astype(BF16)

    sc_sx = proj(PERM_SC)
    prod = sc_sx[:, :W_SC] * sc_sx[:, W_SC:]
    pos = lax.broadcasted_iota(jnp.int32, (tm, 1), 0) % row_len
    left = jnp.where(pos == 0, 0.0, pltpu.roll(prod, 1, axis=0))
    right = jnp.where(pos == row_len - 1, 0.0, pltpu.roll(prod, tm - 1, axis=0))
    conv = scw_ref[0:1, :] * left + scw_ref[1:2, :] * prod + scw_ref[2:3, :] * right
    yc_ref[...] = (f_sb[:, W_FOURIER:] * conv).astype(BF16)

    lx_ref[...] = proj(PERM_LX)
    gl_ref[...] = jax.nn.gelu(proj(PERM_LG)).astype(BF16)

    chunk = 2 * MXU_DIM
    for j in range(gate_ref.shape[1] // chunk):
        gate_ref[:, j * chunk:(j + 1) * chunk] = jax.nn.sigmoid(proj(PERM_G + j * chunk)).astype(BF16)


def _inproj(h, mod, gain, w_in, cdft, sc_w, *, layer, mod_row, tm, row_len):
    nb, t, d = h.shape
    n_cols = w_in.shape[-1]
    n_gate = n_cols - OFF_G
    assert tm % row_len == 0 and t % tm == 0
    row = (lambda b: b) if mod_row is None else (lambda b: mod_row)
    tok = lambda w: pl.BlockSpec((None, tm, w), lambda b, i: (b, i, 0))
    return pl.pallas_call(
        functools.partial(_inproj_kernel, row_len=row_len),
        out_shape=(
            jax.ShapeDtypeStruct((2, t, nb * W_FOURIER), BF16),
            jax.ShapeDtypeStruct((nb, t, W_LRU), F32),
            jax.ShapeDtypeStruct((nb, t, W_LRU), BF16),
            jax.ShapeDtypeStruct((nb, t, W_SC), BF16),
            jax.ShapeDtypeStruct((nb, t, n_gate), BF16),
        ),
        grid=(nb, t // tm),
        in_specs=[
            tok(d),
            pl.BlockSpec((None, None, N_MOD, d), lambda b, i: (layer, row(b), 0, 0)),
            _resident((None, 1, d), lambda b, i: (layer, 0, 0)),
            _resident((None, d, n_cols), lambda b, i: (layer, 0, 0)),
            _resident(cdft.shape, lambda b, i: (0, 0)),
            _resident((None, SC_CONV, W_SC), lambda b, i: (layer, 0, 0)),
        ],
        out_specs=(
            pl.BlockSpec((2, tm, W_FOURIER), lambda b, i: (0, i, b)),
            tok(W_LRU), tok(W_LRU), tok(W_SC), tok(n_gate),
        ),
        compiler_params=_params(("arbitrary", "arbitrary")),
        name="inproj",
    )(h, mod, gain, w_in, cdft, sc_w)


def _scan_layout(t):
    n_seg = 32 if t >= 1024 else 8
    seg_len = -(-t // n_seg)
    while seg_len % 8 != 4:
        seg_len += 1
    return n_seg, seg_len


def _lru_kernel(lx_ref, gl_ref, h0_ref, cw_ref, cb_ref, w_ref, b_ref, lam_ref, y_ref, last_ref,
                xpad, a_f, b_f, a_b, b_b, h_f, h_b, hin_f, hin_b, *, n_seg, seg_len, chunk):
    t, cw = lx_ref.shape
    t_pad = n_seg * seg_len
    halo = SUBLANES

    xpad[0:halo, :] = jnp.zeros((halo, cw), F32)
    xpad[halo + t:2 * halo + t, :] = jnp.zeros((halo, cw), F32)
    xpad[halo:halo + t, :] = lx_ref[...]
    if t_pad > t:
        for ref, val in ((a_f, 1.0), (a_b, 1.0), (b_f, 0.0), (b_b, 0.0)):
            ref[t:t_pad, :] = jnp.full((t_pad - t, cw), val, F32)

    neg_lam = -lam_ref[...]
    softplus = jnp.maximum(neg_lam, 0.0) + jnp.log1p(jnp.exp(-jnp.abs(neg_lam)))
    half_decay = (-0.5 * LRU_C) * softplus

    for k in range(t // chunk):
        first = k * chunk + halo - LRU_CONV // 2
        xc = cb_ref[...]
        for j in range(LRU_CONV):
            xc = xc + cw_ref[j:j + 1, :] * xpad[first + j:first + j + chunk, :]
        half_gates = _dot(xc.astype(BF16), w_ref[...]) + b_ref[...]
        half_x = 0.5 * xc
        rows = slice(k * chunk, (k + 1) * chunk)
        for d, (a_ref, bt_ref) in enumerate(((a_f, b_f), (a_b, b_b))):
            tanh_r = jnp.tanh(half_gates[:, (2 * d) * cw:(2 * d + 1) * cw])
            tanh_i = jnp.tanh(half_gates[:, (2 * d + 1) * cw:(2 * d + 2) * cw])
            log_a = half_decay[d:d + 1, :] + half_decay[d:d + 1, :] * tanh_r
            a = jnp.exp(log_a)
            one_minus_a2 = jnp.tanh(log_a) * (-1.0 - a * a)
            root = one_minus_a2 * lax.rsqrt(jnp.maximum(one_minus_a2, F32_TINY))
            a_ref[rows, :] = a
            bt_ref[rows, :] = root * (half_x + half_x * tanh_i)

    def seg(ref, j):
        return ref[pl.ds(j, n_seg, stride=seg_len), :]

    def summarize(j, carry):
        e_f, p_f, e_b, p_b = carry
        jb = seg_len - 1 - j
        af, ab = seg(a_f, j), seg(a_b, jb)
        return af * e_f + seg(b_f, j), p_f * af, ab * e_b + seg(b_b, jb), p_b * ab

    zeros = jnp.zeros((n_seg, cw), F32)
    ones = jnp.ones((n_seg, cw), F32)
    e_f, p_f, e_b, p_b = lax.fori_loop(0, seg_len, summarize, (zeros, ones, zeros, ones), unroll=SCAN_UNROLL)

    state = h0_ref[0:1, :]
    for s in range(n_seg):
        hin_f[s:s + 1, :] = state
        state = p_f[s:s + 1, :] * state + e_f[s:s + 1, :]
    last_ref[0:1, :] = state
    state = h0_ref[1:2, :]
    for s in reversed(range(n_seg)):
        hin_b[s:s + 1, :] = state
        state = p_b[s:s + 1, :] * state + e_b[s:s + 1, :]
    last_ref[1:2, :] = state

    def emit(j, carry):
        s_f, s_b = carry
        jb = seg_len - 1 - j
        s_f = seg(a_f, j) * s_f + seg(b_f, j)
        s_b = seg(a_b, jb) * s_b + seg(b_b, jb)
        h_f[pl.ds(j, n_seg, stride=seg_len), :] = s_f
        h_b[pl.ds(jb, n_seg, stride=seg_len), :] = s_b
        return s_f, s_b

    lax.fori_loop(0, seg_len, emit, (hin_f[...], hin_b[...]), unroll=SCAN_UNROLL)
    y_ref[...] = ((h_f[0:t, :] + h_b[0:t, :]) * gl_ref[...].astype(F32)).astype(BF16)


def _lru(lx, gl, h0, conv_w, conv_b, w_cat, b_cat, lam, *, layer):
    nb, t, w = lx.shape
    cw = LRU_CW
    n_seg, seg_len = _scan_layout(t)
    t_pad = n_seg * seg_len
    chunk = min(t, 512)
    seq = lambda: pl.BlockSpec((None, t, cw), lambda b, c: (b, 0, c))
    state = lambda: pl.BlockSpec((None, 2, cw), lambda b, c: (b, 0, c))
    scan_buf = pltpu.VMEM((t_pad, cw), F32)
    return pl.pallas_call(
        functools.partial(_lru_kernel, n_seg=n_seg, seg_len=seg_len, chunk=chunk),
        out_shape=(jax.ShapeDtypeStruct((nb, t, w), BF16), jax.ShapeDtypeStruct((nb, 2, w), F32)),
        grid=(nb, w // cw),
        in_specs=[
            seq(), seq(), state(),
            pl.BlockSpec((None, LRU_CONV, cw), lambda b, c: (layer, 0, c)),
            pl.BlockSpec((None, 1, cw), lambda b, c: (layer, 0, c)),
            pl.BlockSpec((None, None, cw, 4 * cw), lambda b, c: (layer, c, 0, 0)),
            pl.BlockSpec((None, None, 1, 4 * cw), lambda b, c: (layer, c, 0, 0)),
            pl.BlockSpec((None, 2, cw), lambda b, c: (layer, 0, c)),
        ],
        out_specs=(seq(), state()),
        scratch_shapes=[pltpu.VMEM((t + 2 * SUBLANES, cw), F32)] + [scan_buf] * 6
        + [pltpu.VMEM((n_seg, cw), F32)] * 2,
        compiler_params=_params(("arbitrary", "arbitrary")),
        name="lru",
    )(lx, gl, h0, conv_w, conv_b, w_cat, b_cat, lam)


def _lru_gate_weights(wa, ba, wx, bx):
    depth, _, heads, blk, _ = wa.shape
    per = LRU_CW // blk
    groups = heads // per
    eye = jnp.eye(per, dtype=wa.dtype)

    def block_diag(w):
        w = w.reshape(depth, 2, groups, per, blk, blk)
        return jnp.einsum("ldgaij,ab->ldgaibj", w, eye).reshape(depth, 2, groups, LRU_CW, LRU_CW)

    w = jnp.stack([block_diag(wa), block_diag(wx)], axis=2)
    w = w.transpose(0, 3, 4, 1, 2, 5).reshape(depth, groups, LRU_CW, 4 * LRU_CW)
    b = jnp.stack([ba.reshape(depth, 2, groups, LRU_CW), bx.reshape(depth, 2, groups, LRU_CW)], axis=2)
    b = b.transpose(0, 3, 1, 2, 4).reshape(depth, groups, 1, 4 * LRU_CW)
    return (0.5 * w).astype(BF16), 0.5 * b


def _seqdft_kernel(d_ref, z_ref, o_ref):
    o_ref[...] = _dot(d_ref[...], z_ref[...]).astype(BF16)


def _seqdft(dmat, z):
    t, k2 = dmat.shape
    n = z.shape[-1]
    tm = min(t, MXU_DIM)
    return pl.pallas_call(
        _seqdft_kernel,
        out_shape=jax.ShapeDtypeStruct((t, n), BF16),
        grid=(t // tm,),
        in_specs=[
            pl.BlockSpec((tm, k2), lambda i: (i, 0)),
            _resident((k2, n), lambda i: (0, 0)),
        ],
        out_specs=pl.BlockSpec((tm, n), lambda i: (i, 0)),
        compiler_params=_params(("arbitrary",)),
        name="seqdft",
    )(dmat, z.reshape(k2, n))


def _channel_dft():
    gw = W_FOURIER // FOURIER_GROUPS
    idx = np.arange(gw)
    ang = 2.0 * np.pi * ((idx[:, None] * idx[None, :]) % gw) / gw
    eye = np.eye(FOURIER_GROUPS)
    cos = np.kron(eye, np.cos(ang)) / np.sqrt(gw)
    sin = np.kron(eye, np.sin(ang)) / np.sqrt(gw)
    return jnp.asarray(np.concatenate([cos, -sin], axis=1), dtype=BF16)


def _sequence_dft(t):
    def table(n_cols, col_step):
        j = lax.broadcasted_iota(jnp.int32, (t, n_cols), 0)
        k = lax.broadcasted_iota(jnp.int32, (t, n_cols), 1) * col_step
        ang = ((j * k) % t).astype(F32) * (2.0 * np.pi / t)
        return jnp.cos(ang), jnp.sin(ang)

    q = 64 if t % 64 == 0 and t > 64 else 1
    c1, s1 = table(t // q, q)
    c2, s2 = table(q, 1)
    cos = c1[:, :, None] * c2[:, None, :] - s1[:, :, None] * s2[:, None, :]
    sin = s1[:, :, None] * c2[:, None, :] + c1[:, :, None] * s2[:, None, :]
    both = jnp.concatenate([cos.reshape(t, t), sin.reshape(t, t)], axis=1)
    return (both * (1.0 / np.sqrt(t))).astype(BF16)


def kernel(x, c, ctx, c_ctx, w_ada, b_ada, g_ffn1, g_mix, g_ffn2, ffn_w_gate, ffn_w_up, ffn_w_down, w_in, lru_conv_w, lru_conv_b, lru_wa, lru_ba, lru_wx, lru_bx, lru_lambda, sc_conv_w, wp_fourier, wp_lru, wp_conv, w_out, g_final):
    depth, d = g_ffn1.shape
    batch, seq, _ = x.shape
    ctx_len = ctx.shape[1]
    ctx_row = batch

    mod = _modulation(c, c_ctx, w_ada, b_ada)
    wg, wu, wd = ffn_w_gate.astype(BF16), ffn_w_up.astype(BF16), ffn_w_down.astype(BF16)
    w_in_b = jnp.concatenate(
        [w_in[..., OFF_F:OFF_LX], w_in[..., OFF_SB:OFF_G], w_in[..., OFF_LX:OFF_SB], w_in[..., OFF_G:]],
        axis=-1).astype(BF16)
    branch_w = tuple(w.astype(BF16) for w in (wp_fourier, wp_lru, wp_conv, w_out))
    gate_w, gate_b = _lru_gate_weights(lru_wa, lru_ba, lru_wx, lru_bx)
    gains = [g.reshape(depth, 1, d) for g in (g_ffn1, g_mix, g_ffn2)]
    g_fin = g_final.reshape(1, d)
    conv_b = lru_conv_b.reshape(depth, 1, W_LRU)
    cdft = _channel_dft()
    dft_lat, dft_ctx = _sequence_dft(seq), _sequence_dft(ctx_len)
    zero_state = jnp.zeros((batch, 2, W_LRU), F32)

    lat = dict(mod_row=None, tm=512)
    cx = dict(mod_row=ctx_row, tm=ctx_len)

    h, hc = x, ctx
    for l in range(depth):
        last = l == depth - 1
        ffn = functools.partial(_ffn, mod=mod, g_final=g_fin, wg=wg, wu=wu, wd=wd, layer=l)
        inproj = functools.partial(_inproj, mod=mod, gain=gains[1], w_in=w_in_b, cdft=cdft, sc_w=sc_conv_w, layer=l)
        lru = functools.partial(_lru, conv_w=lru_conv_w, conv_b=conv_b, w_cat=gate_w, b_cat=gate_b,
                                lam=lru_lambda, layer=l)

        hc = ffn(hc, gain=gains[0], half=0, base=0, **cx)
        h = ffn(h, gain=gains[0], half=0, base=0, **lat)

        zc, lxc, glc, ycc, gatec = inproj(hc, row_len=ctx_len, **cx)
        ylc, ctx_state = lru(lxc, glc, zero_state)
        z, lx, gl, yc, gate = inproj(h, row_len=GRID_W, **lat)
        yl, _ = lru(lx, gl, ctx_state)
        h = ffn(h, gain=gains[2], half=1, base=6, final=last,
                merge=(_seqdft(dft_lat, z), yl, yc, gate, *branch_w), **lat)
        if not last:
            hc = ffn(hc, gain=gains[2], half=1, base=6,
                     merge=(_seqdft(dft_ctx, zc), ylc, ycc, gatec, *branch_w), **cx)
    return h
```

```python
import functools

import numpy as np
import jax
import jax.numpy as jnp
from jax import lax
from jax.experimental import pallas as pl
from jax.experimental.pallas import tpu as pltpu

GRID_W = 64
N_MOD = 9
W_FOURIER = 256
FOURIER_GROUPS = 4
W_LRU = 512
LRU_HEADS = 8
W_SC = 256
LRU_CONV = 4
SC_CONV = 3
LRU_C = 8.0
N_BRANCH = 3
RMS_EPS = 1e-6
OFF_F = 0
OFF_LX = OFF_F + W_FOURIER
OFF_LG = OFF_LX + W_LRU
OFF_SB = OFF_LG + W_LRU
OFF_SC = OFF_SB + W_SC
OFF_SX = OFF_SC + W_SC
OFF_G = OFF_SX + W_SC

LANES = 128
SUBLANES = 8
MXU_DIM = 256
VMEM_LIMIT = 56 * 1024 * 1024
MOD_ROWS = 16
LRU_CW = LANES
BF16 = jnp.bfloat16
F32 = jnp.float32
F32_TINY = float(np.finfo(np.float32).tiny)
SCAN_UNROLL = 4


def _params(sem):
    return pltpu.CompilerParams(dimension_semantics=sem, vmem_limit_bytes=VMEM_LIMIT)


def _resident(block_shape, index_map):
    return pl.BlockSpec(block_shape, index_map, pipeline_mode=pl.Buffered(1))


def _dot(a, b):
    return jnp.dot(a, b, preferred_element_type=F32)


def _rms_mod(x, gain, shift, scale):
    y = x * lax.rsqrt(jnp.mean(x * x, axis=-1, keepdims=True) + RMS_EPS)
    return (y * gain) * (1.0 + scale) + shift


def _mod_kernel(s_ref, w_ref, b_ref, o_ref):
    s = jax.nn.silu(s_ref[...]).astype(BF16)
    part = _dot(s, w_ref[...].astype(BF16))

    @pl.when(pl.program_id(1) == 0)
    def _():
        o_ref[...] = part + b_ref[...]

    @pl.when(pl.program_id(1) > 0)
    def _():
        o_ref[...] += part


def _modulation(c, c_ctx, w_ada, b_ada):
    depth, d, n = w_ada.shape
    batch = c.shape[0]
    rows = jnp.zeros((MOD_ROWS, d), F32).at[:batch].set(c).at[batch].set(c_ctx)
    tk = MXU_DIM
    out = pl.pallas_call(
        _mod_kernel,
        out_shape=jax.ShapeDtypeStruct((depth, MOD_ROWS, n), F32),
        grid=(depth, d // tk),
        in_specs=[
            pl.BlockSpec((MOD_ROWS, tk), lambda l, k: (0, k)),
            pl.BlockSpec((None, tk, n), lambda l, k: (l, k, 0)),
            pl.BlockSpec((None, 1, n), lambda l, k: (l, 0, 0)),
        ],
        out_specs=pl.BlockSpec((None, MOD_ROWS, n), lambda l, k: (l, 0, 0)),
        compiler_params=_params(("arbitrary", "arbitrary")),
        name="modulation",
    )(rows, w_ada, b_ada.reshape(depth, 1, n))
    return out.reshape(depth, MOD_ROWS, N_MOD, d)


def _swiglu_step(x, mod_ref, g_ref, gfin_ref, wg_ref, wu_ref, wd_ref, acc_ref, *, base, chunk, final):
    u = _rms_mod(x, g_ref[...], mod_ref[base:base + 1, :], mod_ref[base + 1:base + 2, :]).astype(BF16)
    d_ff = wg_ref.shape[1]
    for j in range(d_ff // chunk):
        sl = slice(j * chunk, (j + 1) * chunk)
        half_g = _dot(u, wg_ref[:, sl])
        a = ((half_g + half_g * jnp.tanh(half_g)) * _dot(u, wu_ref[:, sl])).astype(BF16)
        part = _dot(a, wd_ref[sl, :])
        if j == 0:
            acc_ref[...] = part
        else:
            acc_ref[...] += part
    y = x + (0.5 * mod_ref[base + 2:base + 3, :]) * acc_ref[...]
    if final:
        y = (y * lax.rsqrt(jnp.mean(y * y, axis=-1, keepdims=True) + RMS_EPS)) * gfin_ref[...]
    return y


def _ffn_kernel(x_ref, mod_ref, *rest, base, final):
    *ffn_refs, o_ref, acc_ref = rest
    o_ref[...] = _swiglu_step(x_ref[...], mod_ref, *ffn_refs, acc_ref, base=base, chunk=MXU_DIM, final=final)


def _mergeffn_kernel(h_ref, mod_ref, yf_lo_ref, yf_hi_ref, yl_ref, yc_ref, gate_ref,
                     wpf_ref, wpl_ref, wpc_ref, wo_ref, *rest, base, final):
    *ffn_refs, o_ref, acc_ref = rest
    tm, d = h_ref.shape
    if yf_lo_ref.shape[0] == tm:
        in_lo = pl.program_id(1) < pl.num_programs(1) // 2
        yf = jnp.where(in_lo, yf_lo_ref[...], yf_hi_ref[...])
    else:
        yf = jnp.concatenate([yf_lo_ref[...], yf_hi_ref[...]], axis=0)
    m = gate_ref[:, 0:d].astype(F32) * _dot(yf, wpf_ref[...])
    m = m + gate_ref[:, d:2 * d].astype(F32) * _dot(yl_ref[...], wpl_ref[...])
    m = m + gate_ref[:, 2 * d:3 * d].astype(F32) * _dot(yc_ref[...], wpc_ref[...])
    h = h_ref[...] + mod_ref[5:6, :] * _dot(m.astype(BF16), wo_ref[...])
    o_ref[...] = _swiglu_step(h, mod_ref, *ffn_refs, acc_ref, base=base, chunk=MXU_DIM, final=final)


def _ffn(h, mod, gain, g_final, wg, wu, wd, *, layer, half, mod_row, base, tm, final=False, merge=None):
    nb, t, d = h.shape
    d_ff = wg.shape[-1]
    row = (lambda b: b) if mod_row is None else (lambda b: mod_row)
    tok = lambda w: pl.BlockSpec((None, tm, w), lambda b, i: (b, i, 0))
    merge_specs, merge_args, body = [], (), _ffn_kernel
    if merge is not None:
        wspec = lambda w: _resident((None, w, d), lambda b, i: (layer, 0, 0))
        half_tiles = max(t // 2 // tm, 1)
        yf_rows = min(tm, t // 2)
        merge_specs = [pl.BlockSpec((yf_rows, W_FOURIER), lambda b, i: (jnp.minimum(i, half_tiles - 1), b)),
                       pl.BlockSpec((yf_rows, W_FOURIER), lambda b, i: (jnp.maximum(i - half_tiles, 0), b)),
                       tok(W_LRU), tok(W_SC), tok(N_BRANCH * d),
                       wspec(W_FOURIER), wspec(W_LRU), wspec(W_SC), wspec(d)]
        merge_args, body = tuple(merge), _mergeffn_kernel
    return pl.pallas_call(
        functools.partial(body, base=base, final=final),
        out_shape=jax.ShapeDtypeStruct(h.shape, F32),
        grid=(nb, t // tm),
        in_specs=[
            tok(d),
            pl.BlockSpec((None, None, N_MOD, d), lambda b, i: (layer, row(b), 0, 0)),
            *merge_specs,
            _resident((None, 1, d), lambda b, i: (layer, 0, 0)),
            _resident((1, d), lambda b, i: (0, 0)),
            _resident((None, None, d, d_ff), lambda b, i: (layer, half, 0, 0)),
            _resident((None, None, d, d_ff), lambda b, i: (layer, half, 0, 0)),
            _resident((None, None, d_ff, d), lambda b, i: (layer, half, 0, 0)),
        ],
        out_specs=tok(d),
        scratch_shapes=[pltpu.VMEM((tm, d), F32)],
        compiler_params=_params(("arbitrary", "arbitrary")),
        name="ffn" if merge is None else "mergeffn",
    )(h, mod, *merge_args, gain, g_final, wg, wu, wd)


def _inproj_kernel(x_ref, mod_ref, g_ref, w_ref, cdft_ref, scw_ref,
                   z_ref, lx_ref, gl_ref, yc_ref, gate_ref, *, row_len):
    tm = x_ref.shape[0]
    u = _rms_mod(x_ref[...], g_ref[...], mod_ref[3:4, :], mod_ref[4:5, :]).astype(BF16)

    half = MXU_DIM
    assert (OFF_LX, OFF_LG, OFF_SB, OFF_SC, OFF_G) == (half, 3 * half, 5 * half, 6 * half, 8 * half)

    def proj(start):
        return _dot(u, w_ref[:, start:start + 2 * half])

    f_lx = proj(OFF_F)
    lx_lg = proj(OFF_LX + half)
    lg_sb = proj(OFF_LG + half)
    sc_sx = proj(OFF_SC)

    z = _dot(f_lx[:, :half].astype(BF16), cdft_ref[...])
    z_ref[0] = z[:, :W_FOURIER].astype(BF16)
    z_ref[1] = z[:, W_FOURIER:].astype(BF16)

    lx_ref[:, :half] = f_lx[:, half:]
    lx_ref[:, half:] = lx_lg[:, :half]
    gl_ref[:, :half] = jax.nn.gelu(lx_lg[:, half:]).astype(BF16)
    gl_ref[:, half:] = jax.nn.gelu(lg_sb[:, :half]).astype(BF16)

    prod = sc_sx[:, :half] * sc_sx[:, half:]
    pos = lax.broadcasted_iota(jnp.int32, (tm, 1), 0) % row_len
    left = jnp.where(pos == 0, 0.0, pltpu.roll(prod, 1, axis=0))
    right = jnp.where(pos == row_len - 1, 0.0, pltpu.roll(prod, tm - 1, axis=0))
    conv = scw_ref[0:1, :] * left + scw_ref[1:2, :] * prod + scw_ref[2:3, :] * right
    yc_ref[...] = (lg_sb[:, half:] * conv).astype(BF16)

    chunk = 2 * half
    for j in range(gate_ref.shape[1] // chunk):
        half_g = proj(OFF_G + j * chunk)
        gate_ref[:, j * chunk:(j + 1) * chunk] = (0.5 + 0.5 * jnp.tanh(half_g)).astype(BF16)


def _inproj(h, mod, gain, w_in, cdft, sc_w, *, layer, mod_row, tm, row_len):
    nb, t, d = h.shape
    n_cols = w_in.shape[-1]
    n_gate = n_cols - OFF_G
    assert tm % row_len == 0 and t % tm == 0
    row = (lambda b: b) if mod_row is None else (lambda b: mod_row)
    tok = lambda w: pl.BlockSpec((None, tm, w), lambda b, i: (b, i, 0))
    return pl.pallas_call(
        functools.partial(_inproj_kernel, row_len=row_len),
        out_shape=(
            jax.ShapeDtypeStruct((2, t, nb * W_FOURIER), BF16),
            jax.ShapeDtypeStruct((nb, t, W_LRU), F32),
            jax.ShapeDtypeStruct((nb, t, W_LRU), BF16),
            jax.ShapeDtypeStruct((nb, t, W_SC), BF16),
            jax.ShapeDtypeStruct((nb, t, n_gate), BF16),
        ),
        grid=(nb, t // tm),
        in_specs=[
            tok(d),
            pl.BlockSpec((None, None, N_MOD, d), lambda b, i: (layer, row(b), 0, 0)),
            _resident((None, 1, d), lambda b, i: (layer, 0, 0)),
            _resident((None, d, n_cols), lambda b, i: (layer, 0, 0)),
            _resident(cdft.shape, lambda b, i: (0, 0)),
            _resident((None, SC_CONV, W_SC), lambda b, i: (layer, 0, 0)),
        ],
        out_specs=(
            pl.BlockSpec((2, tm, W_FOURIER), lambda b, i: (0, i, b)),
            tok(W_LRU), tok(W_LRU), tok(W_SC), tok(n_gate),
        ),
        compiler_params=_params(("arbitrary", "arbitrary")),
        name="inproj",
    )(h, mod, gain, w_in, cdft, sc_w)


def _scan_layout(t):
    n_seg = 32 if t >= 1024 else 8
    seg_len = -(-t // n_seg)
    while seg_len % 8 != 4:
        seg_len += 1
    return n_seg, seg_len


def _lru_kernel(lx_ref, gl_ref, h0_ref, cw_ref, cb_ref, w_ref, b_ref, lam_ref, y_ref, last_ref,
                xpad, a_f, b_f, a_b, b_b, h_f, h_b, hin_f, hin_b, *, n_seg, seg_len, chunk):
    t, cw = lx_ref.shape
    t_pad = n_seg * seg_len
    halo = SUBLANES

    xpad[0:halo, :] = jnp.zeros((halo, cw), F32)
    xpad[halo + t:2 * halo + t, :] = jnp.zeros((halo, cw), F32)
    xpad[halo:halo + t, :] = lx_ref[...]
    if t_pad > t:
        for ref, val in ((a_f, 1.0), (a_b, 1.0), (b_f, 0.0), (b_b, 0.0)):
            ref[t:t_pad, :] = jnp.full((t_pad - t, cw), val, F32)

    neg_lam = -lam_ref[...]
    softplus = jnp.maximum(neg_lam, 0.0) + jnp.log1p(jnp.exp(-jnp.abs(neg_lam)))
    half_decay = (-0.5 * LRU_C) * softplus

    for k in range(t // chunk):
        first = k * chunk + halo - LRU_CONV // 2
        xc = cb_ref[...]
        for j in range(LRU_CONV):
            xc = xc + cw_ref[j:j + 1, :] * xpad[first + j:first + j + chunk, :]
        half_gates = _dot(xc.astype(BF16), w_ref[...]) + b_ref[...]
        half_x = 0.5 * xc
        rows = slice(k * chunk, (k + 1) * chunk)
        for d, (a_ref, bt_ref) in enumerate(((a_f, b_f), (a_b, b_b))):
            tanh_r = jnp.tanh(half_gates[:, (2 * d) * cw:(2 * d + 1) * cw])
            tanh_i = jnp.tanh(half_gates[:, (2 * d + 1) * cw:(2 * d + 2) * cw])
            log_a = half_decay[d:d + 1, :] + half_decay[d:d + 1, :] * tanh_r
            a = jnp.exp(log_a)
            one_minus_a2 = jnp.tanh(log_a) * (-1.0 - a * a)
            root = one_minus_a2 * lax.rsqrt(jnp.maximum(one_minus_a2, F32_TINY))
            a_ref[rows, :] = a
            bt_ref[rows, :] = root * (half_x + half_x * tanh_i)

    def seg(ref, j):
        return ref[pl.ds(j, n_seg, stride=seg_len), :]

    def summarize(j, carry):
        e_f, p_f, e_b, p_b = carry
        jb = seg_len - 1 - j
        af, ab = seg(a_f, j), seg(a_b, jb)
        return af * e_f + seg(b_f, j), p_f * af, ab * e_b + seg(b_b, jb), p_b * ab

    zeros = jnp.zeros((n_seg, cw), F32)
    ones = jnp.ones((n_seg, cw), F32)
    e_f, p_f, e_b, p_b = lax.fori_loop(0, seg_len, summarize, (zeros, ones, zeros, ones), unroll=SCAN_UNROLL)

    state = h0_ref[0:1, :]
    for s in range(n_seg):
        hin_f[s:s + 1, :] = state
        state = p_f[s:s + 1, :] * state + e_f[s:s + 1, :]
    last_ref[0:1, :] = state
    state = h0_ref[1:2, :]
    for s in reversed(range(n_seg)):
        hin_b[s:s + 1, :] = state
        state = p_b[s:s + 1, :] * state + e_b[s:s + 1, :]
    last_ref[1:2, :] = state

    def emit(j, carry):
        s_f, s_b = carry
        jb = seg_len - 1 - j
        s_f = seg(a_f, j) * s_f + seg(b_f, j)
        s_b = seg(a_b, jb) * s_b + seg(b_b, jb)
        h_f[pl.ds(j, n_seg, stride=seg_len), :] = s_f
        h_b[pl.ds(jb, n_seg, stride=seg_len), :] = s_b
        return s_f, s_b

    lax.fori_loop(0, seg_len, emit, (hin_f[...], hin_b[...]), unroll=SCAN_UNROLL)
    y_ref[...] = ((h_f[0:t, :] + h_b[0:t, :]) * gl_ref[...].astype(F32)).astype(BF16)


def _lru(lx, gl, h0, conv_w, conv_b, w_cat, b_cat, lam, *, layer):
    nb, t, w = lx.shape
    cw = LRU_CW
    n_seg, seg_len = _scan_layout(t)
    t_pad = n_seg * seg_len
    chunk = min(t, 512)
    seq = lambda: pl.BlockSpec((None, t, cw), lambda b, c: (b, 0, c))
    state = lambda: pl.BlockSpec((None, 2, cw), lambda b, c: (b, 0, c))
    scan_buf = pltpu.VMEM((t_pad, cw), F32)
    return pl.pallas_call(
        functools.partial(_lru_kernel, n_seg=n_seg, seg_len=seg_len, chunk=chunk),
        out_shape=(jax.ShapeDtypeStruct((nb, t, w), BF16), jax.ShapeDtypeStruct((nb, 2, w), F32)),
        grid=(nb, w // cw),
        in_specs=[
            seq(), seq(), state(),
            pl.BlockSpec((None, LRU_CONV, cw), lambda b, c: (layer, 0, c)),
            pl.BlockSpec((None, 1, cw), lambda b, c: (layer, 0, c)),
            pl.BlockSpec((None, None, cw, 4 * cw), lambda b, c: (layer, c, 0, 0)),
            pl.BlockSpec((None, None, 1, 4 * cw), lambda b, c: (layer, c, 0, 0)),
            pl.BlockSpec((None, 2, cw), lambda b, c: (layer, 0, c)),
        ],
        out_specs=(seq(), state()),
        scratch_shapes=[pltpu.VMEM((t + 2 * SUBLANES, cw), F32)] + [scan_buf] * 6
        + [pltpu.VMEM((n_seg, cw), F32)] * 2,
        compiler_params=_params(("arbitrary", "arbitrary")),
        name="lru",
    )(lx, gl, h0, conv_w, conv_b, w_cat, b_cat, lam)


def _lru_gate_weights(wa, ba, wx, bx):
    depth, _, heads, blk, _ = wa.shape
    per = LRU_CW // blk
    groups = heads // per
    eye = jnp.eye(per, dtype=wa.dtype)

    def block_diag(w):
        w = w.reshape(depth, 2, groups, per, blk, blk)
        return jnp.einsum("ldgaij,ab->ldgaibj", w, eye).reshape(depth, 2, groups, LRU_CW, LRU_CW)

    w = jnp.stack([block_diag(wa), block_diag(wx)], axis=2)
    w = w.transpose(0, 3, 4, 1, 2, 5).reshape(depth, groups, LRU_CW, 4 * LRU_CW)
    b = jnp.stack([ba.reshape(depth, 2, groups, LRU_CW), bx.reshape(depth, 2, groups, LRU_CW)], axis=2)
    b = b.transpose(0, 3, 1, 2, 4).reshape(depth, groups, 1, 4 * LRU_CW)
    return (0.5 * w).astype(BF16), 0.5 * b


DFT_EXTRA_ROWS = 16


def _seqdft_kernel(d_ref, dnext_ref, z_ref, lo_ref, hi_ref):
    tm = d_ref.shape[0]
    t = z_ref.shape[0] // 2
    a = _dot(jnp.concatenate([d_ref[:, :t], dnext_ref[:, :t]], axis=0), z_ref[:t, :])
    b = _dot(jnp.concatenate([d_ref[:, t:], dnext_ref[:, t:]], axis=0), z_ref[t:, :])
    lo_ref[...] = (a[:tm] + b[:tm]).astype(BF16)
    diff = (a - b).astype(BF16)
    row = lax.broadcasted_iota(jnp.int32, (tm, tm + DFT_EXTRA_ROWS), 0)
    col = lax.broadcasted_iota(jnp.int32, (tm, tm + DFT_EXTRA_ROWS), 1)
    flip = jnp.where(col == tm - row, 1.0, 0.0).astype(BF16)
    hi_ref[...] = _dot(flip, diff).astype(BF16)


def _seqdft(dmat, z):
    t = z.shape[1]
    n = z.shape[-1]
    half = t // 2
    tm = min(half, MXU_DIM)
    n_tiles = half // tm
    assert dmat.shape == (half + DFT_EXTRA_ROWS, 2 * t)
    return pl.pallas_call(
        _seqdft_kernel,
        out_shape=(jax.ShapeDtypeStruct((half, n), BF16),) * 2,
        grid=(n_tiles,),
        in_specs=[
            pl.BlockSpec((tm, 2 * t), lambda i: (i, 0)),
            pl.BlockSpec((DFT_EXTRA_ROWS, 2 * t), lambda i: ((i + 1) * (tm // DFT_EXTRA_ROWS), 0)),
            _resident((2 * t, n), lambda i: (0, 0)),
        ],
        out_specs=(pl.BlockSpec((tm, n), lambda i: (i, 0)),
                   pl.BlockSpec((tm, n), lambda i: (n_tiles - 1 - i, 0))),
        compiler_params=_params(("arbitrary",)),
        name="seqdft",
    )(dmat, dmat, z.reshape(2 * t, n))


def _channel_dft():
    gw = W_FOURIER // FOURIER_GROUPS
    idx = np.arange(gw)
    ang = 2.0 * np.pi * ((idx[:, None] * idx[None, :]) % gw) / gw
    eye = np.eye(FOURIER_GROUPS)
    cos = np.kron(eye, np.cos(ang)) / np.sqrt(gw)
    sin = np.kron(eye, np.sin(ang)) / np.sqrt(gw)
    return jnp.asarray(np.concatenate([cos, -sin], axis=1), dtype=BF16)


def _sequence_dft(t):
    rows = t // 2 + DFT_EXTRA_ROWS

    def table(n_cols, col_step):
        j = lax.broadcasted_iota(jnp.int32, (rows, n_cols), 0)
        k = lax.broadcasted_iota(jnp.int32, (rows, n_cols), 1) * col_step
        ang = ((j * k) % t).astype(F32) * (2.0 * np.pi / t)
        return jnp.cos(ang), jnp.sin(ang)

    q = 64 if t % 64 == 0 and t > 64 else 1
    c1, s1 = table(t // q, q)
    c2, s2 = table(q, 1)
    cos = c1[:, :, None] * c2[:, None, :] - s1[:, :, None] * s2[:, None, :]
    sin = s1[:, :, None] * c2[:, None, :] + c1[:, :, None] * s2[:, None, :]
    both = jnp.concatenate([cos.reshape(rows, t), sin.reshape(rows, t)], axis=1)
    return (both * (1.0 / np.sqrt(t))).astype(BF16)


def kernel(x, c, ctx, c_ctx, w_ada, b_ada, g_ffn1, g_mix, g_ffn2, ffn_w_gate, ffn_w_up, ffn_w_down, w_in, lru_conv_w, lru_conv_b, lru_wa, lru_ba, lru_wx, lru_bx, lru_lambda, sc_conv_w, wp_fourier, wp_lru, wp_conv, w_out, g_final):
    depth, d = g_ffn1.shape
    batch, seq, _ = x.shape
    ctx_len = ctx.shape[1]
    ctx_row = batch

    mod = _modulation(c, c_ctx, w_ada, b_ada)
    wg, wu, wd = (0.5 * ffn_w_gate).astype(BF16), ffn_w_up.astype(BF16), ffn_w_down.astype(BF16)
    col_scale = jnp.where(jnp.arange(w_in.shape[-1]) >= OFF_G, 0.5, 1.0).astype(F32)
    w_in_b = (w_in * col_scale).astype(BF16)
    branch_w = tuple(w.astype(BF16) for w in (wp_fourier, wp_lru, wp_conv, w_out))
    gate_w, gate_b = _lru_gate_weights(lru_wa, lru_ba, lru_wx, lru_bx)
    gains = [g.reshape(depth, 1, d) for g in (g_ffn1, g_mix, g_ffn2)]
    g_fin = g_final.reshape(1, d)
    conv_b = lru_conv_b.reshape(depth, 1, W_LRU)
    cdft = _channel_dft()
    dft_lat, dft_ctx = _sequence_dft(seq), _sequence_dft(ctx_len)
    zero_state = jnp.zeros((batch, 2, W_LRU), F32)

    lat = dict(mod_row=None, tm=512)
    cx = dict(mod_row=ctx_row, tm=ctx_len)

    h, hc = x, ctx
    for l in range(depth):
        last = l == depth - 1
        ffn = functools.partial(_ffn, mod=mod, g_final=g_fin, wg=wg, wu=wu, wd=wd, layer=l)
        inproj = functools.partial(_inproj, mod=mod, gain=gains[1], w_in=w_in_b, cdft=cdft, sc_w=sc_conv_w, layer=l)
        lru = functools.partial(_lru, conv_w=lru_conv_w, conv_b=conv_b, w_cat=gate_w, b_cat=gate_b,
                                lam=lru_lambda, layer=l)

        hc = ffn(hc, gain=gains[0], half=0, base=0, **cx)
        h = ffn(h, gain=gains[0], half=0, base=0, **lat)

        zc, lxc, glc, ycc, gatec = inproj(hc, row_len=ctx_len, **cx)
        ylc, ctx_state = lru(lxc, glc, zero_state)
        z, lx, gl, yc, gate = inproj(h, row_len=GRID_W, **lat)
        yl, _ = lru(lx, gl, ctx_state)
        h = ffn(h, gain=gains[2], half=1, base=6, final=last,
                merge=(*_seqdft(dft_lat, z), yl, yc, gate, *branch_w), **lat)
        if not last:
            hc = ffn(hc, gain=gains[2], half=1, base=6,
                     merge=(*_seqdft(dft_ctx, zc), ylc, ycc, gatec, *branch_w), **cx)
    return h
```

```python
import functools

import numpy as np
import jax
import jax.numpy as jnp
from jax import lax
from jax.experimental import pallas as pl
from jax.experimental.pallas import tpu as pltpu

GRID_W = 64
N_MOD = 9
W_FOURIER = 256
FOURIER_GROUPS = 4
W_LRU = 512
LRU_HEADS = 8
W_SC = 256
LRU_CONV = 4
SC_CONV = 3
LRU_C = 8.0
N_BRANCH = 3
RMS_EPS = 1e-6
OFF_F = 0
OFF_LX = OFF_F + W_FOURIER
OFF_LG = OFF_LX + W_LRU
OFF_SB = OFF_LG + W_LRU
OFF_SC = OFF_SB + W_SC
OFF_SX = OFF_SC + W_SC
OFF_G = OFF_SX + W_SC

LANES = 128
SUBLANES = 8
MXU_DIM = 256
VMEM_LIMIT = 56 * 1024 * 1024
MOD_ROWS = 16
LRU_CW = LANES
BF16 = jnp.bfloat16
F32 = jnp.float32
F32_TINY = float(np.finfo(np.float32).tiny)
SCAN_UNROLL = 4
LRU_CHUNK = 512


def _params(sem):
    return pltpu.CompilerParams(dimension_semantics=sem, vmem_limit_bytes=VMEM_LIMIT)


def _resident(block_shape, index_map):
    return pl.BlockSpec(block_shape, index_map, pipeline_mode=pl.Buffered(1))


def _dot(a, b):
    return jnp.dot(a, b, preferred_element_type=F32)


def _rms_mod(x, gain, shift, scale):
    y = x * lax.rsqrt(jnp.mean(x * x, axis=-1, keepdims=True) + RMS_EPS)
    return (y * gain) * (1.0 + scale) + shift


def _mod_kernel(s_ref, w_ref, b_ref, o_ref):
    s = jax.nn.silu(s_ref[...]).astype(BF16)
    part = _dot(s, w_ref[...].astype(BF16))

    @pl.when(pl.program_id(1) == 0)
    def _():
        o_ref[...] = part + b_ref[...]

    @pl.when(pl.program_id(1) > 0)
    def _():
        o_ref[...] += part


def _modulation(c, c_ctx, w_ada, b_ada):
    depth, d, n = w_ada.shape
    batch = c.shape[0]
    rows = jnp.zeros((MOD_ROWS, d), F32).at[:batch].set(c).at[batch].set(c_ctx)
    tk = MXU_DIM
    out = pl.pallas_call(
        _mod_kernel,
        out_shape=jax.ShapeDtypeStruct((depth, MOD_ROWS, n), F32),
        grid=(depth, d // tk),
        in_specs=[
            pl.BlockSpec((MOD_ROWS, tk), lambda l, k: (0, k)),
            pl.BlockSpec((None, tk, n), lambda l, k: (l, k, 0)),
            pl.BlockSpec((None, 1, n), lambda l, k: (l, 0, 0)),
        ],
        out_specs=pl.BlockSpec((None, MOD_ROWS, n), lambda l, k: (l, 0, 0)),
        compiler_params=_params(("arbitrary", "arbitrary")),
        name="modulation",
    )(rows, w_ada, b_ada.reshape(depth, 1, n))
    return out.reshape(depth, MOD_ROWS, N_MOD, d)


def _swiglu_step(x, mod_ref, g_ref, gfin_ref, wg_ref, wu_ref, wd_ref, acc_ref, *, base, chunk, final):
    u = _rms_mod(x, g_ref[...], mod_ref[base:base + 1, :], mod_ref[base + 1:base + 2, :]).astype(BF16)
    d_ff = wg_ref.shape[1]
    for j in range(d_ff // chunk):
        sl = slice(j * chunk, (j + 1) * chunk)
        half_g = _dot(u, wg_ref[:, sl])
        a = ((half_g + half_g * jnp.tanh(half_g)) * _dot(u, wu_ref[:, sl])).astype(BF16)
        part = _dot(a, wd_ref[sl, :])
        if j == 0:
            acc_ref[...] = part
        else:
            acc_ref[...] += part
    y = x + (0.5 * mod_ref[base + 2:base + 3, :]) * acc_ref[...]
    if final:
        y = (y * lax.rsqrt(jnp.mean(y * y, axis=-1, keepdims=True) + RMS_EPS)) * gfin_ref[...]
    return y


def _ffn_kernel(x_ref, mod_ref, *rest, base, final):
    *ffn_refs, o_ref, acc_ref = rest
    o_ref[...] = _swiglu_step(x_ref[...], mod_ref, *ffn_refs, acc_ref, base=base, chunk=MXU_DIM, final=final)


def _mergeffn_kernel(h_ref, mod_ref, yf_lo_ref, yf_hi_ref, yl_ref, yc_ref, gate_ref,
                     wpf_ref, wpl_ref, wpc_ref, wo_ref, *rest, base, final):
    *ffn_refs, o_ref, acc_ref = rest
    tm, d = h_ref.shape
    if yf_lo_ref.shape[0] == tm:
        in_lo = pl.program_id(1) < pl.num_programs(1) // 2
        yf = jnp.where(in_lo, yf_lo_ref[...], yf_hi_ref[...])
    else:
        pieces = []
        for s in range(yf_lo_ref.shape[1] // W_FOURIER):
            cols = slice(s * W_FOURIER, (s + 1) * W_FOURIER)
            pieces += [yf_lo_ref[:, cols], yf_hi_ref[:, cols]]
        yf = jnp.concatenate(pieces, axis=0)
    m = gate_ref[:, 0:d].astype(F32) * _dot(yf, wpf_ref[...])
    m = m + gate_ref[:, d:2 * d].astype(F32) * _dot(yl_ref[...], wpl_ref[...])
    m = m + gate_ref[:, 2 * d:3 * d].astype(F32) * _dot(yc_ref[...], wpc_ref[...])
    h = h_ref[...] + mod_ref[5:6, :] * _dot(m.astype(BF16), wo_ref[...])
    o_ref[...] = _swiglu_step(h, mod_ref, *ffn_refs, acc_ref, base=base, chunk=MXU_DIM, final=final)


def _ffn(h, mod, gain, g_final, wg, wu, wd, *, layer, half, mod_row, base, tm, final=False, merge=None,
         seq_len=None):
    nb, t, d = h.shape
    d_ff = wg.shape[-1]
    row = (lambda b: b) if mod_row is None else (lambda b: mod_row)
    tok = lambda w: pl.BlockSpec((None, tm, w), lambda b, i: (b, i, 0))
    merge_specs, merge_args, body = [], (), _ffn_kernel
    if merge is not None:
        wspec = lambda w: _resident((None, w, d), lambda b, i: (layer, 0, 0))
        seq_len = t if seq_len is None else seq_len
        if tm < seq_len:
            assert t == seq_len
            half_tiles = seq_len // 2 // tm
            yf_lo = pl.BlockSpec((tm, W_FOURIER), lambda b, i: (jnp.minimum(i, half_tiles - 1), b))
            yf_hi = pl.BlockSpec((tm, W_FOURIER), lambda b, i: (jnp.maximum(i - half_tiles, 0), b))
        else:
            per_tile, per_row = tm // seq_len, t // seq_len
            yf_lo = yf_hi = pl.BlockSpec((seq_len // 2, per_tile * W_FOURIER),
                                         lambda b, i: (0, (b * per_row) // per_tile + i))
        merge_specs = [yf_lo, yf_hi,
                       tok(W_LRU), tok(W_SC), tok(N_BRANCH * d),
                       wspec(W_FOURIER), wspec(W_LRU), wspec(W_SC), wspec(d)]
        merge_args, body = tuple(merge), _mergeffn_kernel
    return pl.pallas_call(
        functools.partial(body, base=base, final=final),
        out_shape=jax.ShapeDtypeStruct(h.shape, F32),
        grid=(nb, t // tm),
        in_specs=[
            tok(d),
            pl.BlockSpec((None, None, N_MOD, d), lambda b, i: (layer, row(b), 0, 0)),
            *merge_specs,
            _resident((None, 1, d), lambda b, i: (layer, 0, 0)),
            _resident((1, d), lambda b, i: (0, 0)),
            _resident((None, None, d, d_ff), lambda b, i: (layer, half, 0, 0)),
            _resident((None, None, d, d_ff), lambda b, i: (layer, half, 0, 0)),
            _resident((None, None, d_ff, d), lambda b, i: (layer, half, 0, 0)),
        ],
        out_specs=tok(d),
        scratch_shapes=[pltpu.VMEM((tm, d), F32)],
        compiler_params=_params(("arbitrary", "arbitrary")),
        name="ffn" if merge is None else "mergeffn",
    )(h, mod, *merge_args, gain, g_final, wg, wu, wd)


def _inproj_kernel(x_ref, mod_ref, g_ref, w_ref, cdft_ref, scw_ref,
                   z_ref, lx_ref, gl_ref, yc_ref, gate_ref, *, row_len):
    tm = x_ref.shape[0]
    u = _rms_mod(x_ref[...], g_ref[...], mod_ref[3:4, :], mod_ref[4:5, :]).astype(BF16)

    half = MXU_DIM
    assert (OFF_LX, OFF_LG, OFF_SB, OFF_SC, OFF_G) == (half, 3 * half, 5 * half, 6 * half, 8 * half)

    def proj(start):
        return _dot(u, w_ref[:, start:start + 2 * half])

    f_lx = proj(OFF_F)
    lx_lg = proj(OFF_LX + half)
    lg_sb = proj(OFF_LG + half)
    sc_sx = proj(OFF_SC)

    z = _dot(f_lx[:, :half].astype(BF16), cdft_ref[...])
    z_ref[0] = z[:, :W_FOURIER].astype(BF16)
    z_ref[1] = z[:, W_FOURIER:].astype(BF16)

    lx_ref[:, :half] = f_lx[:, half:]
    lx_ref[:, half:] = lx_lg[:, :half]
    gl_ref[:, :half] = jax.nn.gelu(lx_lg[:, half:]).astype(BF16)
    gl_ref[:, half:] = jax.nn.gelu(lg_sb[:, :half]).astype(BF16)

    prod = sc_sx[:, :half] * sc_sx[:, half:]
    pos = lax.broadcasted_iota(jnp.int32, (tm, 1), 0) % row_len
    left = jnp.where(pos == 0, 0.0, pltpu.roll(prod, 1, axis=0))
    right = jnp.where(pos == row_len - 1, 0.0, pltpu.roll(prod, tm - 1, axis=0))
    conv = scw_ref[0:1, :] * left + scw_ref[1:2, :] * prod + scw_ref[2:3, :] * right
    yc_ref[...] = (lg_sb[:, half:] * conv).astype(BF16)

    chunk = 2 * half
    for j in range(gate_ref.shape[1] // chunk):
        half_g = proj(OFF_G + j * chunk)
        gate_ref[:, j * chunk:(j + 1) * chunk] = (0.5 + 0.5 * jnp.tanh(half_g)).astype(BF16)


def _inproj(h, mod, gain, w_in, cdft, sc_w, *, layer, mod_row, tm, row_len):
    nb, t, d = h.shape
    n_cols = w_in.shape[-1]
    n_gate = n_cols - OFF_G
    assert tm % row_len == 0 and t % tm == 0
    row = (lambda b: b) if mod_row is None else (lambda b: mod_row)
    tok = lambda w: pl.BlockSpec((None, tm, w), lambda b, i: (b, i, 0))
    return pl.pallas_call(
        functools.partial(_inproj_kernel, row_len=row_len),
        out_shape=(
            jax.ShapeDtypeStruct((2, t, nb * W_FOURIER), BF16),
            jax.ShapeDtypeStruct((nb, t, W_LRU), F32),
            jax.ShapeDtypeStruct((nb, t, W_LRU), BF16),
            jax.ShapeDtypeStruct((nb, t, W_SC), BF16),
            jax.ShapeDtypeStruct((nb, t, n_gate), BF16),
        ),
        grid=(nb, t // tm),
        in_specs=[
            tok(d),
            pl.BlockSpec((None, None, N_MOD, d), lambda b, i: (layer, row(b), 0, 0)),
            _resident((None, 1, d), lambda b, i: (layer, 0, 0)),
            _resident((None, d, n_cols), lambda b, i: (layer, 0, 0)),
            _resident(cdft.shape, lambda b, i: (0, 0)),
            _resident((None, SC_CONV, W_SC), lambda b, i: (layer, 0, 0)),
        ],
        out_specs=(
            pl.BlockSpec((2, tm, W_FOURIER), lambda b, i: (0, i, b)),
            tok(W_LRU), tok(W_LRU), tok(W_SC), tok(n_gate),
        ),
        compiler_params=_params(("arbitrary", "arbitrary")),
        name="inproj",
    )(h, mod, gain, w_in, cdft, sc_w)


def _scan_layout(t):
    n_seg = 32 if t >= 1024 else 8
    seg_len = -(-t // n_seg)
    while seg_len % 8 != 4:
        seg_len += 1
    return n_seg, seg_len


def _lru_sequence(lx_ref, gl_ref, y_ref, h0, cw_ref, cb_ref, w_ref, b_ref, lam_ref,
                  xpad, a_f, b_f, a_b, b_b, h_f, h_b, hin_f, hin_b):
    t, cw = lx_ref.shape
    n_seg, seg_len = _scan_layout(t)
    chunk = min(t, LRU_CHUNK)
    t_pad = n_seg * seg_len
    halo = SUBLANES

    xpad[0:halo, :] = jnp.zeros((halo, cw), F32)
    xpad[halo + t:2 * halo + t, :] = jnp.zeros((halo, cw), F32)
    xpad[halo:halo + t, :] = lx_ref[...]
    if t_pad > t:
        for ref, val in ((a_f, 1.0), (a_b, 1.0), (b_f, 0.0), (b_b, 0.0)):
            ref[t:t_pad, :] = jnp.full((t_pad - t, cw), val, F32)

    neg_lam = -lam_ref[...]
    softplus = jnp.maximum(neg_lam, 0.0) + jnp.log1p(jnp.exp(-jnp.abs(neg_lam)))
    half_decay = (-0.5 * LRU_C) * softplus

    for k in range(t // chunk):
        first = k * chunk + halo - LRU_CONV // 2
        xc = cb_ref[...]
        for j in range(LRU_CONV):
            xc = xc + cw_ref[j:j + 1, :] * xpad[first + j:first + j + chunk, :]
        half_gates = _dot(xc.astype(BF16), w_ref[...]) + b_ref[...]
        half_x = 0.5 * xc
        rows = slice(k * chunk, (k + 1) * chunk)
        for d, (a_ref, bt_ref) in enumerate(((a_f, b_f), (a_b, b_b))):
            tanh_r = jnp.tanh(half_gates[:, (2 * d) * cw:(2 * d + 1) * cw])
            tanh_i = jnp.tanh(half_gates[:, (2 * d + 1) * cw:(2 * d + 2) * cw])
            log_a = half_decay[d:d + 1, :] + half_decay[d:d + 1, :] * tanh_r
            a = jnp.exp(log_a)
            one_minus_a2 = jnp.tanh(log_a) * (-1.0 - a * a)
            root = one_minus_a2 * lax.rsqrt(jnp.maximum(one_minus_a2, F32_TINY))
            a_ref[rows, :] = a
            bt_ref[rows, :] = root * (half_x + half_x * tanh_i)

    def seg(ref, j):
        return ref[pl.ds(j, n_seg, stride=seg_len), :]

    def summarize(j, carry):
        e_f, p_f, e_b, p_b = carry
        jb = seg_len - 1 - j
        af, ab = seg(a_f, j), seg(a_b, jb)
        return af * e_f + seg(b_f, j), p_f * af, ab * e_b + seg(b_b, jb), p_b * ab

    zeros = jnp.zeros((n_seg, cw), F32)
    ones = jnp.ones((n_seg, cw), F32)
    e_f, p_f, e_b, p_b = lax.fori_loop(0, seg_len, summarize, (zeros, ones, zeros, ones), unroll=SCAN_UNROLL)

    last_f, last_b = h0
    for s in range(n_seg):
        hin_f[s:s + 1, :] = last_f
        last_f = p_f[s:s + 1, :] * last_f + e_f[s:s + 1, :]
    for s in reversed(range(n_seg)):
        hin_b[s:s + 1, :] = last_b
        last_b = p_b[s:s + 1, :] * last_b + e_b[s:s + 1, :]

    def emit(j, carry):
        s_f, s_b = carry
        jb = seg_len - 1 - j
        s_f = seg(a_f, j) * s_f + seg(b_f, j)
        s_b = seg(a_b, jb) * s_b + seg(b_b, jb)
        h_f[pl.ds(j, n_seg, stride=seg_len), :] = s_f
        h_b[pl.ds(jb, n_seg, stride=seg_len), :] = s_b
        return s_f, s_b

    lax.fori_loop(0, seg_len, emit, (hin_f[0:n_seg, :], hin_b[0:n_seg, :]), unroll=SCAN_UNROLL)
    y_ref[...] = ((h_f[0:t, :] + h_b[0:t, :]) * gl_ref[...].astype(F32)).astype(BF16)
    return last_f, last_b


def _lru_kernel(lxc_ref, glc_ref, lx_ref, gl_ref, *rest):
    *param_refs, yc_ref, y_ref = rest[:7]
    scratch = rest[7:]
    zero = jnp.zeros((1, lx_ref.shape[1]), F32)
    ctx_state = _lru_sequence(lxc_ref, glc_ref, yc_ref, (zero, zero), *param_refs, *scratch)
    _lru_sequence(lx_ref, gl_ref, y_ref, ctx_state, *param_refs, *scratch)


def _lru(lxc, glc, lx, gl, conv_w, conv_b, w_cat, b_cat, lam, *, layer):
    nb, t, w = lx.shape
    tc = lxc.shape[1]
    cw = LRU_CW
    n_seg, seg_len = _scan_layout(t)
    t_pad = n_seg * seg_len
    seq = lambda rows: pl.BlockSpec((None, rows, cw), lambda b, c: (b, 0, c))
    scan_buf = pltpu.VMEM((t_pad, cw), F32)
    return pl.pallas_call(
        _lru_kernel,
        out_shape=(jax.ShapeDtypeStruct((nb, tc, w), BF16), jax.ShapeDtypeStruct((nb, t, w), BF16)),
        grid=(nb, w // cw),
        in_specs=[
            seq(tc), seq(tc), seq(t), seq(t),
            pl.BlockSpec((None, LRU_CONV, cw), lambda b, c: (layer, 0, c)),
            pl.BlockSpec((None, 1, cw), lambda b, c: (layer, 0, c)),
            pl.BlockSpec((None, None, cw, 4 * cw), lambda b, c: (layer, c, 0, 0)),
            pl.BlockSpec((None, None, 1, 4 * cw), lambda b, c: (layer, c, 0, 0)),
            pl.BlockSpec((None, 2, cw), lambda b, c: (layer, 0, c)),
        ],
        out_specs=(seq(tc), seq(t)),
        scratch_shapes=[pltpu.VMEM((t + 2 * SUBLANES, cw), F32)] + [scan_buf] * 6
        + [pltpu.VMEM((n_seg, cw), F32)] * 2,
        compiler_params=_params(("arbitrary", "arbitrary")),
        name="lru",
    )(lxc, glc, lx, gl, conv_w, conv_b, w_cat, b_cat, lam)


def _lru_gate_weights(wa, ba, wx, bx):
    depth, _, heads, blk, _ = wa.shape
    per = LRU_CW // blk
    groups = heads // per
    eye = jnp.eye(per, dtype=wa.dtype)

    def block_diag(w):
        w = w.reshape(depth, 2, groups, per, blk, blk)
        return jnp.einsum("ldgaij,ab->ldgaibj", w, eye).reshape(depth, 2, groups, LRU_CW, LRU_CW)

    w = jnp.stack([block_diag(wa), block_diag(wx)], axis=2)
    w = w.transpose(0, 3, 4, 1, 2, 5).reshape(depth, groups, LRU_CW, 4 * LRU_CW)
    b = jnp.stack([ba.reshape(depth, 2, groups, LRU_CW), bx.reshape(depth, 2, groups, LRU_CW)], axis=2)
    b = b.transpose(0, 3, 1, 2, 4).reshape(depth, groups, 1, 4 * LRU_CW)
    return (0.5 * w).astype(BF16), 0.5 * b


DFT_EXTRA_ROWS = 16


def _seqdft_kernel(d_ref, dnext_ref, z_ref, lo_ref, hi_ref):
    tm = d_ref.shape[0]
    t = z_ref.shape[0] // 2
    a = _dot(jnp.concatenate([d_ref[:, :t], dnext_ref[:, :t]], axis=0), z_ref[:t, :])
    b = _dot(jnp.concatenate([d_ref[:, t:], dnext_ref[:, t:]], axis=0), z_ref[t:, :])
    lo_ref[...] = (a[:tm] + b[:tm]).astype(BF16)
    diff = (a - b).astype(BF16)
    row = lax.broadcasted_iota(jnp.int32, (tm, tm + DFT_EXTRA_ROWS), 0)
    col = lax.broadcasted_iota(jnp.int32, (tm, tm + DFT_EXTRA_ROWS), 1)
    flip = jnp.where(col == tm - row, 1.0, 0.0).astype(BF16)
    hi_ref[...] = _dot(flip, diff).astype(BF16)


def _seqdft(dmat, z):
    t = z.shape[1]
    n = z.shape[-1]
    half = t // 2
    tm = min(half, MXU_DIM)
    n_tiles = half // tm
    assert dmat.shape == (half + DFT_EXTRA_ROWS, 2 * t)
    return pl.pallas_call(
        _seqdft_kernel,
        out_shape=(jax.ShapeDtypeStruct((half, n), BF16),) * 2,
        grid=(n_tiles,),
        in_specs=[
            pl.BlockSpec((tm, 2 * t), lambda i: (i, 0)),
            pl.BlockSpec((DFT_EXTRA_ROWS, 2 * t), lambda i: ((i + 1) * (tm // DFT_EXTRA_ROWS), 0)),
            _resident((2 * t, n), lambda i: (0, 0)),
        ],
        out_specs=(pl.BlockSpec((tm, n), lambda i: (i, 0)),
                   pl.BlockSpec((tm, n), lambda i: (n_tiles - 1 - i, 0))),
        compiler_params=_params(("arbitrary",)),
        name="seqdft",
    )(dmat, dmat, z.reshape(2 * t, n))


def _channel_dft():
    gw = W_FOURIER // FOURIER_GROUPS
    idx = np.arange(gw)
    ang = 2.0 * np.pi * ((idx[:, None] * idx[None, :]) % gw) / gw
    eye = np.eye(FOURIER_GROUPS)
    cos = np.kron(eye, np.cos(ang)) / np.sqrt(gw)
    sin = np.kron(eye, np.sin(ang)) / np.sqrt(gw)
    return jnp.asarray(np.concatenate([cos, -sin], axis=1), dtype=F32).astype(BF16)


def _sequence_dft(t):
    rows = t // 2 + DFT_EXTRA_ROWS

    def table(n_cols, col_step):
        j = lax.broadcasted_iota(jnp.int32, (rows, n_cols), 0)
        k = lax.broadcasted_iota(jnp.int32, (rows, n_cols), 1) * col_step
        ang = ((j * k) % t).astype(F32) * (2.0 * np.pi / t)
        return jnp.cos(ang), jnp.sin(ang)

    q = 64 if t % 64 == 0 and t > 64 else 1
    c1, s1 = table(t // q, q)
    c2, s2 = table(q, 1)
    cos = c1[:, :, None] * c2[:, None, :] - s1[:, :, None] * s2[:, None, :]
    sin = s1[:, :, None] * c2[:, None, :] + c1[:, :, None] * s2[:, None, :]
    both = jnp.concatenate([cos.reshape(rows, t), sin.reshape(rows, t)], axis=1)
    return (both * (1.0 / np.sqrt(t))).astype(BF16)


def kernel(x, c, ctx, c_ctx, w_ada, b_ada, g_ffn1, g_mix, g_ffn2, ffn_w_gate, ffn_w_up, ffn_w_down, w_in, lru_conv_w, lru_conv_b, lru_wa, lru_ba, lru_wx, lru_bx, lru_lambda, sc_conv_w, wp_fourier, wp_lru, wp_conv, w_out, g_final):
    depth, d = g_ffn1.shape
    batch, seq, _ = x.shape
    ctx_len = ctx.shape[1]
    ctx_row = batch

    mod = _modulation(c, c_ctx, w_ada, b_ada)
    wg, wu, wd = (0.5 * ffn_w_gate).astype(BF16), ffn_w_up.astype(BF16), ffn_w_down.astype(BF16)
    col_scale = jnp.where(jnp.arange(w_in.shape[-1]) >= OFF_G, 0.5, 1.0).astype(F32)
    w_in_b = (w_in * col_scale).astype(BF16)
    branch_w = tuple(w.astype(BF16) for w in (wp_fourier, wp_lru, wp_conv, w_out))
    gate_w, gate_b = _lru_gate_weights(lru_wa, lru_ba, lru_wx, lru_bx)
    gains = [g.reshape(depth, 1, d) for g in (g_ffn1, g_mix, g_ffn2)]
    g_fin = g_final.reshape(1, d)
    conv_b = lru_conv_b.reshape(depth, 1, W_LRU)
    cdft = _channel_dft()
    dft_lat, dft_ctx = _sequence_dft(seq), _sequence_dft(ctx_len)

    lat_ffn = dict(mod_row=None, tm=1024)
    lat_merge = dict(mod_row=None, tm=512)
    lat_proj = dict(mod_row=None, tm=1024)
    cx_flat = dict(mod_row=ctx_row, tm=min(1024, batch * ctx_len))
    cx_merge = dict(mod_row=ctx_row, tm=min(512, batch * ctx_len))
    cx_proj = dict(mod_row=ctx_row, tm=ctx_len)

    h, hc = x, ctx
    for l in range(depth):
        last = l == depth - 1
        ffn = functools.partial(_ffn, mod=mod, g_final=g_fin, wg=wg, wu=wu, wd=wd, layer=l)
        inproj = functools.partial(_inproj, mod=mod, gain=gains[1], w_in=w_in_b, cdft=cdft, sc_w=sc_conv_w, layer=l)
        lru = functools.partial(_lru, conv_w=lru_conv_w, conv_b=conv_b, w_cat=gate_w, b_cat=gate_b,
                                lam=lru_lambda, layer=l)

        hc = ffn(hc.reshape(1, batch * ctx_len, d), gain=gains[0], half=0, base=0,
                 **cx_flat).reshape(batch, ctx_len, d)
        h = ffn(h, gain=gains[0], half=0, base=0, **lat_ffn)

        zc, lxc, glc, ycc, gatec = inproj(hc, row_len=ctx_len, **cx_proj)
        z, lx, gl, yc, gate = inproj(h, row_len=GRID_W, **lat_proj)
        ylc, yl = lru(lxc, glc, lx, gl)
        h = ffn(h, gain=gains[2], half=1, base=6, final=last,
                merge=(*_seqdft(dft_lat, z), yl, yc, gate, *branch_w), **lat_merge)
        if not last:
            flat = lambda a: a.reshape(1, batch * ctx_len, a.shape[-1])
            hc = ffn(flat(hc), gain=gains[2], half=1, base=6, seq_len=ctx_len,
                     merge=(*_seqdft(dft_ctx, zc), flat(ylc), flat(ycc), flat(gatec), *branch_w),
                     **cx_merge).reshape(batch, ctx_len, d)
    return h
```

```python
import functools

import numpy as np
import jax
import jax.numpy as jnp
from jax import lax
from jax.experimental import pallas as pl
from jax.experimental.pallas import tpu as pltpu

GRID_W = 64
N_MOD = 9
W_FOURIER = 256
FOURIER_GROUPS = 4
W_LRU = 512
LRU_HEADS = 8
W_SC = 256
LRU_CONV = 4
SC_CONV = 3
LRU_C = 8.0
N_BRANCH = 3
RMS_EPS = 1e-6
OFF_F = 0
OFF_LX = OFF_F + W_FOURIER
OFF_LG = OFF_LX + W_LRU
OFF_SB = OFF_LG + W_LRU
OFF_SC = OFF_SB + W_SC
OFF_SX = OFF_SC + W_SC
OFF_G = OFF_SX + W_SC

LANES = 128
SUBLANES = 8
MXU_DIM = 256
VMEM_LIMIT = 56 * 1024 * 1024
MOD_ROWS = 16
LRU_CW = LANES
BF16 = jnp.bfloat16
F32 = jnp.float32
F32_TINY = float(np.finfo(np.float32).tiny)
SCAN_UNROLL = 4
LRU_CHUNK = 512


def _params(sem):
    return pltpu.CompilerParams(dimension_semantics=sem, vmem_limit_bytes=VMEM_LIMIT)


def _resident(block_shape, index_map):
    return pl.BlockSpec(block_shape, index_map, pipeline_mode=pl.Buffered(1))


def _dot(a, b):
    return jnp.dot(a, b, preferred_element_type=F32)


def _rms_mod(x, gain, shift, scale):
    y = x * lax.rsqrt(jnp.mean(x * x, axis=-1, keepdims=True) + RMS_EPS)
    return y * (gain * (1.0 + scale)) + shift


def _mod_kernel(s_ref, w_ref, b_ref, o_ref):
    s = jax.nn.silu(s_ref[...]).astype(BF16)
    part = _dot(s, w_ref[...].astype(BF16))

    @pl.when(pl.program_id(1) == 0)
    def _():
        o_ref[...] = part + b_ref[...]

    @pl.when(pl.program_id(1) > 0)
    def _():
        o_ref[...] += part


def _modulation(c, c_ctx, w_ada, b_ada):
    depth, d, n = w_ada.shape
    batch = c.shape[0]
    rows = jnp.zeros((MOD_ROWS, d), F32).at[:batch].set(c).at[batch].set(c_ctx)
    tk = 2 * MXU_DIM
    out = pl.pallas_call(
        _mod_kernel,
        out_shape=jax.ShapeDtypeStruct((depth, MOD_ROWS, n), F32),
        grid=(depth, d // tk),
        in_specs=[
            pl.BlockSpec((MOD_ROWS, tk), lambda l, k: (0, k)),
            pl.BlockSpec((None, tk, n), lambda l, k: (l, k, 0)),
            pl.BlockSpec((None, 1, n), lambda l, k: (l, 0, 0)),
        ],
        out_specs=pl.BlockSpec((None, MOD_ROWS, n), lambda l, k: (l, 0, 0)),
        compiler_params=_params(("arbitrary", "arbitrary")),
        name="modulation",
    )(rows, w_ada, b_ada.reshape(depth, 1, n))
    return out.reshape(depth, MOD_ROWS, N_MOD, d)


def _swiglu_step(x, mod_ref, g_ref, gfin_ref, wg_ref, wu_ref, wd_ref, acc_ref, *, base, chunk, final):
    u = _rms_mod(x, g_ref[...], mod_ref[base:base + 1, :], mod_ref[base + 1:base + 2, :]).astype(BF16)
    d_ff = wg_ref.shape[1]
    for j in range(d_ff // chunk):
        sl = slice(j * chunk, (j + 1) * chunk)
        half_g = _dot(u, wg_ref[:, sl])
        a = ((half_g + half_g * jnp.tanh(half_g)) * _dot(u, wu_ref[:, sl])).astype(BF16)
        part = _dot(a, wd_ref[sl, :])
        if j == 0:
            acc_ref[...] = part
        else:
            acc_ref[...] += part
    y = x + (0.5 * mod_ref[base + 2:base + 3, :]) * acc_ref[...]
    if final:
        y = (y * lax.rsqrt(jnp.mean(y * y, axis=-1, keepdims=True) + RMS_EPS)) * gfin_ref[...]
    return y


def _ffn_kernel(x_ref, mod_ref, *rest, base, final):
    *ffn_refs, o_ref, acc_ref = rest
    o_ref[...] = _swiglu_step(x_ref[...], mod_ref, *ffn_refs, acc_ref, base=base, chunk=MXU_DIM, final=final)


def _mergeffn_kernel(h_ref, mod_ref, yf_lo_ref, yf_hi_ref, yl_ref, yc_ref, gate_ref,
                     wpf_ref, wpl_ref, wpc_ref, wo_ref, *rest, base, final):
    *ffn_refs, o_ref, acc_ref = rest
    tm, d = h_ref.shape
    if yf_lo_ref.shape[0] == tm:
        in_lo = pl.program_id(1) < pl.num_programs(1) // 2
        yf = jnp.where(in_lo, yf_lo_ref[...], yf_hi_ref[...])
    else:
        pieces = []
        for s in range(yf_lo_ref.shape[1] // W_FOURIER):
            cols = slice(s * W_FOURIER, (s + 1) * W_FOURIER)
            pieces += [yf_lo_ref[:, cols], yf_hi_ref[:, cols]]
        yf = jnp.concatenate(pieces, axis=0)
    m = gate_ref[:, 0:d].astype(F32) * _dot(yf, wpf_ref[...])
    m = m + gate_ref[:, d:2 * d].astype(F32) * _dot(yl_ref[...], wpl_ref[...])
    m = m + gate_ref[:, 2 * d:3 * d].astype(F32) * _dot(yc_ref[...], wpc_ref[...])
    h = h_ref[...] + mod_ref[5:6, :] * _dot(m.astype(BF16), wo_ref[...])
    o_ref[...] = _swiglu_step(h, mod_ref, *ffn_refs, acc_ref, base=base, chunk=MXU_DIM, final=final)


def _ffn(h, mod, gain, g_final, wg, wu, wd, *, layer, half, mod_row, base, tm, final=False, merge=None,
         seq_len=None):
    nb, t, d = h.shape
    d_ff = wg.shape[-1]
    row = (lambda b: b) if mod_row is None else (lambda b: mod_row)
    tok = lambda w: pl.BlockSpec((None, tm, w), lambda b, i: (b, i, 0))
    merge_specs, merge_args, body = [], (), _ffn_kernel
    if merge is not None:
        wspec = lambda w: _resident((None, w, d), lambda b, i: (layer, 0, 0))
        seq_len = t if seq_len is None else seq_len
        if tm < seq_len:
            assert t == seq_len
            half_tiles = seq_len // 2 // tm
            yf_lo = pl.BlockSpec((tm, W_FOURIER), lambda b, i: (jnp.minimum(i, half_tiles - 1), b))
            yf_hi = pl.BlockSpec((tm, W_FOURIER), lambda b, i: (jnp.maximum(i - half_tiles, 0), b))
        else:
            per_tile, per_row = tm // seq_len, t // seq_len
            yf_lo = yf_hi = pl.BlockSpec((seq_len // 2, per_tile * W_FOURIER),
                                         lambda b, i: (0, (b * per_row) // per_tile + i))
        merge_specs = [yf_lo, yf_hi,
                       tok(W_LRU), tok(W_SC), tok(N_BRANCH * d),
                       wspec(W_FOURIER), wspec(W_LRU), wspec(W_SC), wspec(d)]
        merge_args, body = tuple(merge), _mergeffn_kernel
    return pl.pallas_call(
        functools.partial(body, base=base, final=final),
        out_shape=jax.ShapeDtypeStruct(h.shape, F32),
        grid=(nb, t // tm),
        in_specs=[
            tok(d),
            pl.BlockSpec((None, None, N_MOD, d), lambda b, i: (layer, row(b), 0, 0)),
            *merge_specs,
            _resident((None, 1, d), lambda b, i: (layer, 0, 0)),
            _resident((1, d), lambda b, i: (0, 0)),
            _resident((None, None, d, d_ff), lambda b, i: (layer, half, 0, 0)),
            _resident((None, None, d, d_ff), lambda b, i: (layer, half, 0, 0)),
            _resident((None, None, d_ff, d), lambda b, i: (layer, half, 0, 0)),
        ],
        out_specs=tok(d),
        scratch_shapes=[pltpu.VMEM((tm, d), F32)],
        compiler_params=_params(("arbitrary", "arbitrary")),
        name="ffn" if merge is None else "mergeffn",
    )(h, mod, *merge_args, gain, g_final, wg, wu, wd)


def _inproj_kernel(x_ref, mod_ref, g_ref, w_ref, cdft_ref, scw_ref,
                   z_ref, lx_ref, gl_ref, yc_ref, gate_ref, *, row_len):
    tm = x_ref.shape[0]
    u = _rms_mod(x_ref[...], g_ref[...], mod_ref[3:4, :], mod_ref[4:5, :]).astype(BF16)

    half = MXU_DIM
    assert (OFF_LX, OFF_LG, OFF_SB, OFF_SC, OFF_G) == (half, 3 * half, 5 * half, 6 * half, 8 * half)

    def proj(start):
        return _dot(u, w_ref[:, start:start + 2 * half])

    f_lx = proj(OFF_F)
    lx_lg = proj(OFF_LX + half)
    lg_sb = proj(OFF_LG + half)
    sc_sx = proj(OFF_SC)

    z = _dot(f_lx[:, :half].astype(BF16), cdft_ref[...])
    z_ref[0] = z[:, :W_FOURIER].astype(BF16)
    z_ref[1] = z[:, W_FOURIER:].astype(BF16)

    lx_ref[:, :half] = f_lx[:, half:]
    lx_ref[:, half:] = lx_lg[:, :half]
    gl_ref[:, :half] = jax.nn.gelu(lx_lg[:, half:]).astype(BF16)
    gl_ref[:, half:] = jax.nn.gelu(lg_sb[:, :half]).astype(BF16)

    prod = sc_sx[:, :half] * sc_sx[:, half:]
    pos = lax.broadcasted_iota(jnp.int32, (tm, 1), 0) % row_len
    left = jnp.where(pos == 0, 0.0, pltpu.roll(prod, 1, axis=0))
    right = jnp.where(pos == row_len - 1, 0.0, pltpu.roll(prod, tm - 1, axis=0))
    conv = scw_ref[0:1, :] * left + scw_ref[1:2, :] * prod + scw_ref[2:3, :] * right
    yc_ref[...] = (lg_sb[:, half:] * conv).astype(BF16)

    chunk = 2 * half
    for j in range(gate_ref.shape[1] // chunk):
        half_g = proj(OFF_G + j * chunk)
        gate_ref[:, j * chunk:(j + 1) * chunk] = (0.5 + 0.5 * jnp.tanh(half_g)).astype(BF16)


def _inproj(h, mod, gain, w_in, cdft, sc_w, *, layer, mod_row, tm, row_len):
    nb, t, d = h.shape
    n_cols = w_in.shape[-1]
    n_gate = n_cols - OFF_G
    assert tm % row_len == 0 and t % tm == 0
    row = (lambda b: b) if mod_row is None else (lambda b: mod_row)
    tok = lambda w: pl.BlockSpec((None, tm, w), lambda b, i: (b, i, 0))
    return pl.pallas_call(
        functools.partial(_inproj_kernel, row_len=row_len),
        out_shape=(
            jax.ShapeDtypeStruct((2, t, nb * W_FOURIER), BF16),
            jax.ShapeDtypeStruct((nb, t, W_LRU), F32),
            jax.ShapeDtypeStruct((nb, t, W_LRU), BF16),
            jax.ShapeDtypeStruct((nb, t, W_SC), BF16),
            jax.ShapeDtypeStruct((nb, t, n_gate), BF16),
        ),
        grid=(nb, t // tm),
        in_specs=[
            tok(d),
            pl.BlockSpec((None, None, N_MOD, d), lambda b, i: (layer, row(b), 0, 0)),
            _resident((None, 1, d), lambda b, i: (layer, 0, 0)),
            _resident((None, d, n_cols), lambda b, i: (layer, 0, 0)),
            _resident(cdft.shape, lambda b, i: (0, 0)),
            _resident((None, SC_CONV, W_SC), lambda b, i: (layer, 0, 0)),
        ],
        out_specs=(
            pl.BlockSpec((2, tm, W_FOURIER), lambda b, i: (0, i, b)),
            tok(W_LRU), tok(W_LRU), tok(W_SC), tok(n_gate),
        ),
        compiler_params=_params(("arbitrary", "arbitrary")),
        name="inproj",
    )(h, mod, gain, w_in, cdft, sc_w)


def _scan_layout(t):
    n_seg = 32 if t >= 1024 else 8
    seg_len = -(-t // n_seg)
    while seg_len % 8 != 4:
        seg_len += 1
    return n_seg, seg_len


def _lru_sequence(lx_ref, gl_ref, y_ref, h0, cw_ref, cb_ref, w_ref, b_ref, lam_ref,
                  xpad, a_f, b_f, a_b, b_b, h_f, h_b, hin_f, hin_b):
    t, cw = lx_ref.shape
    n_seg, seg_len = _scan_layout(t)
    chunk = min(t, LRU_CHUNK)
    t_pad = n_seg * seg_len
    halo = SUBLANES

    xpad[0:halo, :] = jnp.zeros((halo, cw), F32)
    xpad[halo + t:2 * halo + t, :] = jnp.zeros((halo, cw), F32)
    xpad[halo:halo + t, :] = lx_ref[...]
    if t_pad > t:
        for ref, val in ((a_f, 1.0), (a_b, 1.0), (b_f, 0.0), (b_b, 0.0)):
            ref[t:t_pad, :] = jnp.full((t_pad - t, cw), val, F32)

    neg_lam = -lam_ref[...]
    softplus = jnp.maximum(neg_lam, 0.0) + jnp.log1p(jnp.exp(-jnp.abs(neg_lam)))
    half_decay = (-0.5 * LRU_C) * softplus

    for k in range(t // chunk):
        first = k * chunk + halo - LRU_CONV // 2
        half_x = cb_ref[...]
        for j in range(LRU_CONV):
            half_x = half_x + cw_ref[j:j + 1, :] * xpad[first + j:first + j + chunk, :]
        half_gates = _dot(half_x.astype(BF16), w_ref[...]) + b_ref[...]
        rows = slice(k * chunk, (k + 1) * chunk)
        for d, (a_ref, bt_ref) in enumerate(((a_f, b_f), (a_b, b_b))):
            tanh_r = jnp.tanh(half_gates[:, (2 * d) * cw:(2 * d + 1) * cw])
            tanh_i = jnp.tanh(half_gates[:, (2 * d + 1) * cw:(2 * d + 2) * cw])
            log_a = half_decay[d:d + 1, :] + half_decay[d:d + 1, :] * tanh_r
            a = jnp.exp(log_a)
            one_minus_a2 = jnp.tanh(log_a) * (-1.0 - a * a)
            root = one_minus_a2 * lax.rsqrt(jnp.maximum(one_minus_a2, F32_TINY))
            a_ref[rows, :] = a
            bt_ref[rows, :] = root * (half_x + half_x * tanh_i)

    def seg(ref, j):
        return ref[pl.ds(j, n_seg, stride=seg_len), :]

    def summarize(j, carry):
        e_f, p_f, e_b, p_b = carry
        jb = seg_len - 1 - j
        af, ab = seg(a_f, j), seg(a_b, jb)
        return af * e_f + seg(b_f, j), p_f * af, ab * e_b + seg(b_b, jb), p_b * ab

    zeros = jnp.zeros((n_seg, cw), F32)
    ones = jnp.ones((n_seg, cw), F32)
    e_f, p_f, e_b, p_b = lax.fori_loop(0, seg_len, summarize, (zeros, ones, zeros, ones), unroll=SCAN_UNROLL)

    last_f, last_b = h0
    for s in range(n_seg):
        hin_f[s:s + 1, :] = last_f
        last_f = p_f[s:s + 1, :] * last_f + e_f[s:s + 1, :]
    for s in reversed(range(n_seg)):
        hin_b[s:s + 1, :] = last_b
        last_b = p_b[s:s + 1, :] * last_b + e_b[s:s + 1, :]

    def emit(j, carry):
        s_f, s_b = carry
        jb = seg_len - 1 - j
        s_f = seg(a_f, j) * s_f + seg(b_f, j)
        s_b = seg(a_b, jb) * s_b + seg(b_b, jb)
        h_f[pl.ds(j, n_seg, stride=seg_len), :] = s_f
        h_b[pl.ds(jb, n_seg, stride=seg_len), :] = s_b
        return s_f, s_b

    lax.fori_loop(0, seg_len, emit, (hin_f[0:n_seg, :], hin_b[0:n_seg, :]), unroll=SCAN_UNROLL)
    y_ref[...] = ((h_f[0:t, :] + h_b[0:t, :]) * gl_ref[...].astype(F32)).astype(BF16)
    return last_f, last_b


def _lru_kernel(lxc_ref, glc_ref, lx_ref, gl_ref, *rest):
    *param_refs, yc_ref, y_ref = rest[:7]
    scratch = rest[7:]
    zero = jnp.zeros((1, lx_ref.shape[1]), F32)
    ctx_state = _lru_sequence(lxc_ref, glc_ref, yc_ref, (zero, zero), *param_refs, *scratch)
    _lru_sequence(lx_ref, gl_ref, y_ref, ctx_state, *param_refs, *scratch)


def _lru(lxc, glc, lx, gl, conv_w, conv_b, w_cat, b_cat, lam, *, layer):
    nb, t, w = lx.shape
    tc = lxc.shape[1]
    cw = LRU_CW
    n_seg, seg_len = _scan_layout(t)
    t_pad = n_seg * seg_len
    seq = lambda rows: pl.BlockSpec((None, rows, cw), lambda b, c: (b, 0, c))
    scan_buf = pltpu.VMEM((t_pad, cw), F32)
    return pl.pallas_call(
        _lru_kernel,
        out_shape=(jax.ShapeDtypeStruct((nb, tc, w), BF16), jax.ShapeDtypeStruct((nb, t, w), BF16)),
        grid=(nb, w // cw),
        in_specs=[
            seq(tc), seq(tc), seq(t), seq(t),
            pl.BlockSpec((None, LRU_CONV, cw), lambda b, c: (layer, 0, c)),
            pl.BlockSpec((None, 1, cw), lambda b, c: (layer, 0, c)),
            pl.BlockSpec((None, None, cw, 4 * cw), lambda b, c: (layer, c, 0, 0)),
            pl.BlockSpec((None, None, 1, 4 * cw), lambda b, c: (layer, c, 0, 0)),
            pl.BlockSpec((None, 2, cw), lambda b, c: (layer, 0, c)),
        ],
        out_specs=(seq(tc), seq(t)),
        scratch_shapes=[pltpu.VMEM((t + 2 * SUBLANES, cw), F32)] + [scan_buf] * 6
        + [pltpu.VMEM((n_seg, cw), F32)] * 2,
        compiler_params=_params(("arbitrary", "arbitrary")),
        name="lru",
    )(lxc, glc, lx, gl, conv_w, conv_b, w_cat, b_cat, lam)


def _lru_gate_weights(wa, ba, wx, bx):
    depth, _, heads, blk, _ = wa.shape
    per = LRU_CW // blk
    groups = heads // per
    eye = jnp.eye(per, dtype=wa.dtype)

    def block_diag(w):
        w = w.reshape(depth, 2, groups, per, blk, blk)
        return jnp.einsum("ldgaij,ab->ldgaibj", w, eye).reshape(depth, 2, groups, LRU_CW, LRU_CW)

    w = jnp.stack([block_diag(wa), block_diag(wx)], axis=2)
    w = w.transpose(0, 3, 4, 1, 2, 5).reshape(depth, groups, LRU_CW, 4 * LRU_CW)
    b = jnp.stack([ba.reshape(depth, 2, groups, LRU_CW), bx.reshape(depth, 2, groups, LRU_CW)], axis=2)
    b = b.transpose(0, 3, 1, 2, 4).reshape(depth, groups, 1, 4 * LRU_CW)
    return w.astype(BF16), 0.5 * b


DFT_EXTRA_ROWS = 16


def _seqdft_kernel(d_ref, dnext_ref, z_ref, lo_ref, hi_ref):
    tm = d_ref.shape[0]
    t = z_ref.shape[0] // 2
    a = _dot(jnp.concatenate([d_ref[:, :t], dnext_ref[:, :t]], axis=0), z_ref[:t, :])
    b = _dot(jnp.concatenate([d_ref[:, t:], dnext_ref[:, t:]], axis=0), z_ref[t:, :])
    lo_ref[...] = (a[:tm] + b[:tm]).astype(BF16)
    diff = (a - b).astype(BF16)
    row = lax.broadcasted_iota(jnp.int32, (tm, tm + DFT_EXTRA_ROWS), 0)
    col = lax.broadcasted_iota(jnp.int32, (tm, tm + DFT_EXTRA_ROWS), 1)
    flip = jnp.where(col == tm - row, 1.0, 0.0).astype(BF16)
    hi_ref[...] = _dot(flip, diff).astype(BF16)


def _seqdft(dmat, z):
    t = z.shape[1]
    n = z.shape[-1]
    half = t // 2
    tm = min(half, MXU_DIM)
    n_tiles = half // tm
    assert dmat.shape == (half + DFT_EXTRA_ROWS, 2 * t)
    return pl.pallas_call(
        _seqdft_kernel,
        out_shape=(jax.ShapeDtypeStruct((half, n), BF16),) * 2,
        grid=(n_tiles,),
        in_specs=[
            pl.BlockSpec((tm, 2 * t), lambda i: (i, 0)),
            pl.BlockSpec((DFT_EXTRA_ROWS, 2 * t), lambda i: ((i + 1) * (tm // DFT_EXTRA_ROWS), 0)),
            _resident((2 * t, n), lambda i: (0, 0)),
        ],
        out_specs=(pl.BlockSpec((tm, n), lambda i: (i, 0)),
                   pl.BlockSpec((tm, n), lambda i: (n_tiles - 1 - i, 0))),
        compiler_params=_params(("arbitrary",)),
        name="seqdft",
    )(dmat, dmat, z.reshape(2 * t, n))


def _channel_dft():
    gw = W_FOURIER // FOURIER_GROUPS
    idx = np.arange(gw)
    ang = 2.0 * np.pi * ((idx[:, None] * idx[None, :]) % gw) / gw
    eye = np.eye(FOURIER_GROUPS)
    cos = np.kron(eye, np.cos(ang)) / np.sqrt(gw)
    sin = np.kron(eye, np.sin(ang)) / np.sqrt(gw)
    return jnp.asarray(np.concatenate([cos, -sin], axis=1), dtype=F32).astype(BF16)


def _sequence_dft(t):
    rows = t // 2 + DFT_EXTRA_ROWS

    def table(n_cols, col_step):
        j = lax.broadcasted_iota(jnp.int32, (rows, n_cols), 0)
        k = lax.broadcasted_iota(jnp.int32, (rows, n_cols), 1) * col_step
        ang = ((j * k) % t).astype(F32) * (2.0 * np.pi / t)
        return jnp.cos(ang), jnp.sin(ang)

    q = 64 if t % 64 == 0 and t > 64 else 1
    c1, s1 = table(t // q, q)
    c2, s2 = table(q, 1)
    cos = c1[:, :, None] * c2[:, None, :] - s1[:, :, None] * s2[:, None, :]
    sin = s1[:, :, None] * c2[:, None, :] + c1[:, :, None] * s2[:, None, :]
    both = jnp.concatenate([cos.reshape(rows, t), sin.reshape(rows, t)], axis=1)
    return (both * (1.0 / np.sqrt(t))).astype(BF16)


def kernel(x, c, ctx, c_ctx, w_ada, b_ada, g_ffn1, g_mix, g_ffn2, ffn_w_gate, ffn_w_up, ffn_w_down, w_in, lru_conv_w, lru_conv_b, lru_wa, lru_ba, lru_wx, lru_bx, lru_lambda, sc_conv_w, wp_fourier, wp_lru, wp_conv, w_out, g_final):
    depth, d = g_ffn1.shape
    batch, seq, _ = x.shape
    ctx_len = ctx.shape[1]
    ctx_row = batch

    mod = _modulation(c, c_ctx, w_ada, b_ada)
    wg, wu, wd = (0.5 * ffn_w_gate).astype(BF16), ffn_w_up.astype(BF16), ffn_w_down.astype(BF16)
    col_scale = jnp.where(jnp.arange(w_in.shape[-1]) >= OFF_G, 0.5, 1.0).astype(F32)
    w_in_b = (w_in * col_scale).astype(BF16)
    branch_w = tuple(w.astype(BF16) for w in (wp_fourier, wp_lru, wp_conv, w_out))
    gate_w, gate_b = _lru_gate_weights(lru_wa, lru_ba, lru_wx, lru_bx)
    gains = [g.reshape(depth, 1, d) for g in (g_ffn1, g_mix, g_ffn2)]
    g_fin = g_final.reshape(1, d)
    half_conv_w, half_conv_b = 0.5 * lru_conv_w, 0.5 * lru_conv_b.reshape(depth, 1, W_LRU)
    cdft = _channel_dft()
    dft_lat, dft_ctx = _sequence_dft(seq), _sequence_dft(ctx_len)

    lat_ffn = dict(mod_row=None, tm=1024)
    lat_merge = dict(mod_row=None, tm=512)
    lat_proj = dict(mod_row=None, tm=1024)
    cx_flat = dict(mod_row=ctx_row, tm=min(1024, batch * ctx_len))
    cx_merge = dict(mod_row=ctx_row, tm=min(512, batch * ctx_len))
    cx_proj = dict(mod_row=ctx_row, tm=ctx_len)

    h, hc = x, ctx
    for l in range(depth):
        last = l == depth - 1
        ffn = functools.partial(_ffn, mod=mod, g_final=g_fin, wg=wg, wu=wu, wd=wd, layer=l)
        inproj = functools.partial(_inproj, mod=mod, gain=gains[1], w_in=w_in_b, cdft=cdft, sc_w=sc_conv_w, layer=l)
        lru = functools.partial(_lru, conv_w=half_conv_w, conv_b=half_conv_b, w_cat=gate_w, b_cat=gate_b,
                                lam=lru_lambda, layer=l)

        hc = ffn(hc.reshape(1, batch * ctx_len, d), gain=gains[0], half=0, base=0,
                 **cx_flat).reshape(batch, ctx_len, d)
        h = ffn(h, gain=gains[0], half=0, base=0, **lat_ffn)

        zc, lxc, glc, ycc, gatec = inproj(hc, row_len=ctx_len, **cx_proj)
        z, lx, gl, yc, gate = inproj(h, row_len=GRID_W, **lat_proj)
        ylc, yl = lru(lxc, glc, lx, gl)
        h = ffn(h, gain=gains[2], half=1, base=6, final=last,
                merge=(*_seqdft(dft_lat, z), yl, yc, gate, *branch_w), **lat_merge)
        if not last:
            flat = lambda a: a.reshape(1, batch * ctx_len, a.shape[-1])
            hc = ffn(flat(hc), gain=gains[2], half=1, base=6, seq_len=ctx_len,
                     merge=(*_seqdft(dft_ctx, zc), flat(ylc), flat(ycc), flat(gatec), *branch_w),
                     **cx_merge).reshape(batch, ctx_len, d)
    return h
```

```python
import functools

import numpy as np
import jax
import jax.numpy as jnp
from jax import lax
from jax.experimental import pallas as pl
from jax.experimental.pallas import tpu as pltpu

GRID_W = 64
N_MOD = 9
W_FOURIER = 256
FOURIER_GROUPS = 4
W_LRU = 512
LRU_HEADS = 8
W_SC = 256
LRU_CONV = 4
SC_CONV = 3
LRU_C = 8.0
N_BRANCH = 3
RMS_EPS = 1e-6
OFF_F = 0
OFF_LX = OFF_F + W_FOURIER
OFF_LG = OFF_LX + W_LRU
OFF_SB = OFF_LG + W_LRU
OFF_SC = OFF_SB + W_SC
OFF_SX = OFF_SC + W_SC
OFF_G = OFF_SX + W_SC

LANES = 128
SUBLANES = 8
MXU_DIM = 256
VMEM_LIMIT = 56 * 1024 * 1024
MOD_ROWS = 16
LRU_CW = LANES
BF16 = jnp.bfloat16
F32 = jnp.float32
F32_TINY = float(np.finfo(np.float32).tiny)
SCAN_UNROLL = 4
LRU_CHUNK = 512


def _params(sem):
    return pltpu.CompilerParams(dimension_semantics=sem, vmem_limit_bytes=VMEM_LIMIT)


def _resident(block_shape, index_map):
    return pl.BlockSpec(block_shape, index_map, pipeline_mode=pl.Buffered(1))


def _dot(a, b):
    return jnp.dot(a, b, preferred_element_type=F32)


def _rms_mod(x, gain, shift, scale):
    y = x * lax.rsqrt(jnp.mean(x * x, axis=-1, keepdims=True) + RMS_EPS)
    return y * (gain * (1.0 + scale)) + shift


def _mod_kernel(s_ref, w_ref, b_ref, o_ref):
    s = jax.nn.silu(s_ref[...]).astype(BF16)
    part = _dot(s, w_ref[...].astype(BF16))

    @pl.when(pl.program_id(1) == 0)
    def _():
        o_ref[...] = part + b_ref[...]

    @pl.when(pl.program_id(1) > 0)
    def _():
        o_ref[...] += part


def _modulation(c, c_ctx, w_ada, b_ada):
    depth, d, n = w_ada.shape
    batch = c.shape[0]
    rows = jnp.zeros((MOD_ROWS, d), F32).at[:batch].set(c).at[batch].set(c_ctx)
    tk = MXU_DIM
    out = pl.pallas_call(
        _mod_kernel,
        out_shape=jax.ShapeDtypeStruct((depth, MOD_ROWS, n), F32),
        grid=(depth, d // tk),
        in_specs=[
            pl.BlockSpec((MOD_ROWS, tk), lambda l, k: (0, k)),
            pl.BlockSpec((None, tk, n), lambda l, k: (l, k, 0)),
            pl.BlockSpec((None, 1, n), lambda l, k: (l, 0, 0)),
        ],
        out_specs=pl.BlockSpec((None, MOD_ROWS, n), lambda l, k: (l, 0, 0)),
        compiler_params=_params(("arbitrary", "arbitrary")),
        name="modulation",
    )(rows, w_ada, b_ada.reshape(depth, 1, n))
    return out.reshape(depth, MOD_ROWS, N_MOD, d)


def _swiglu_step(x, mod_ref, g_ref, gfin_ref, wg_ref, wu_ref, wd_ref, acc_ref, *, base, chunk, final):
    u = _rms_mod(x, g_ref[...], mod_ref[base:base + 1, :], mod_ref[base + 1:base + 2, :]).astype(BF16)
    d_ff = wg_ref.shape[1]
    assert d_ff // chunk >= 2
    for j in range(d_ff // chunk):
        sl = slice(j * chunk, (j + 1) * chunk)
        half_g = _dot(u, wg_ref[:, sl])
        a = ((half_g + half_g * jnp.tanh(half_g)) * _dot(u, wu_ref[:, sl])).astype(BF16)
        part = _dot(a, wd_ref[sl, :])
        if j == 0:
            acc_ref[...] = part
        elif j < d_ff // chunk - 1:
            acc_ref[...] += part
    y = x + (0.5 * mod_ref[base + 2:base + 3, :]) * (acc_ref[...] + part)
    if final:
        y = (y * lax.rsqrt(jnp.mean(y * y, axis=-1, keepdims=True) + RMS_EPS)) * gfin_ref[...]
    return y


def _ffn_kernel(x_ref, mod_ref, *rest, base, final):
    *ffn_refs, o_ref, acc_ref = rest
    o_ref[...] = _swiglu_step(x_ref[...], mod_ref, *ffn_refs, acc_ref, base=base, chunk=MXU_DIM, final=final)


def _mergeffn_kernel(h_ref, mod_ref, yf_lo_ref, yf_hi_ref, yl_ref, yc_ref, gate_ref,
                     wpf_ref, wpl_ref, wpc_ref, wo_ref, *rest, base, final):
    *ffn_refs, o_ref, acc_ref = rest
    tm, d = h_ref.shape
    if yf_lo_ref.shape[0] == tm:
        in_lo = pl.program_id(1) < pl.num_programs(1) // 2
        yf = jnp.where(in_lo, yf_lo_ref[...], yf_hi_ref[...])
    else:
        pieces = []
        for s in range(yf_lo_ref.shape[1] // W_FOURIER):
            cols = slice(s * W_FOURIER, (s + 1) * W_FOURIER)
            pieces += [yf_lo_ref[:, cols], yf_hi_ref[:, cols]]
        yf = jnp.concatenate(pieces, axis=0)
    m = gate_ref[:, 0:d].astype(F32) * _dot(yf, wpf_ref[...])
    m = m + gate_ref[:, d:2 * d].astype(F32) * _dot(yl_ref[...], wpl_ref[...])
    m = m + gate_ref[:, 2 * d:3 * d].astype(F32) * _dot(yc_ref[...], wpc_ref[...])
    h = h_ref[...] + mod_ref[5:6, :] * _dot(m.astype(BF16), wo_ref[...])
    o_ref[...] = _swiglu_step(h, mod_ref, *ffn_refs, acc_ref, base=base, chunk=MXU_DIM, final=final)


def _ffn(h, mod, gain, g_final, wg, wu, wd, *, layer, half, mod_row, base, tm, final=False, merge=None,
         seq_len=None):
    nb, t, d = h.shape
    d_ff = wg.shape[-1]
    row = (lambda b: b) if mod_row is None else (lambda b: mod_row)
    tok = lambda w: pl.BlockSpec((None, tm, w), lambda b, i: (b, i, 0))
    merge_specs, merge_args, body = [], (), _ffn_kernel
    if merge is not None:
        wspec = lambda w: _resident((None, w, d), lambda b, i: (layer, 0, 0))
        seq_len = t if seq_len is None else seq_len
        if tm < seq_len:
            assert t == seq_len
            half_tiles = seq_len // 2 // tm
            yf_lo = pl.BlockSpec((tm, W_FOURIER), lambda b, i: (jnp.minimum(i, half_tiles - 1), b))
            yf_hi = pl.BlockSpec((tm, W_FOURIER), lambda b, i: (jnp.maximum(i - half_tiles, 0), b))
        else:
            per_tile, per_row = tm // seq_len, t // seq_len
            yf_lo = yf_hi = pl.BlockSpec((seq_len // 2, per_tile * W_FOURIER),
                                         lambda b, i: (0, (b * per_row) // per_tile + i))
        merge_specs = [yf_lo, yf_hi,
                       tok(W_LRU), tok(W_SC), tok(N_BRANCH * d),
                       wspec(W_FOURIER), wspec(W_LRU), wspec(W_SC), wspec(d)]
        merge_args, body = tuple(merge), _mergeffn_kernel
    return pl.pallas_call(
        functools.partial(body, base=base, final=final),
        out_shape=jax.ShapeDtypeStruct(h.shape, F32),
        grid=(nb, t // tm),
        in_specs=[
            tok(d),
            pl.BlockSpec((None, None, N_MOD, d), lambda b, i: (layer, row(b), 0, 0)),
            *merge_specs,
            _resident((None, 1, d), lambda b, i: (layer, 0, 0)),
            _resident((1, d), lambda b, i: (0, 0)),
            _resident((None, None, d, d_ff), lambda b, i: (layer, half, 0, 0)),
            _resident((None, None, d, d_ff), lambda b, i: (layer, half, 0, 0)),
            _resident((None, None, d_ff, d), lambda b, i: (layer, half, 0, 0)),
        ],
        out_specs=tok(d),
        scratch_shapes=[pltpu.VMEM((tm, d), F32)],
        compiler_params=_params(("arbitrary", "arbitrary")),
        name="ffn" if merge is None else "mergeffn",
    )(h, mod, *merge_args, gain, g_final, wg, wu, wd)


def _inproj_kernel(x_ref, mod_ref, g_ref, w_ref, cdft_ref, scw_ref,
                   z_ref, lx_ref, gl_ref, yc_ref, gate_ref, *, row_len):
    tm = x_ref.shape[0]
    u = _rms_mod(x_ref[...], g_ref[...], mod_ref[3:4, :], mod_ref[4:5, :]).astype(BF16)

    half = MXU_DIM
    assert (OFF_LX, OFF_LG, OFF_SB, OFF_SC, OFF_G) == (half, 3 * half, 5 * half, 6 * half, 8 * half)

    def proj(start):
        return _dot(u, w_ref[:, start:start + 2 * half])

    f_lx = proj(OFF_F)
    lx_lg = proj(OFF_LX + half)
    lg_sb = proj(OFF_LG + half)
    sc_sx = proj(OFF_SC)

    z = _dot(f_lx[:, :half].astype(BF16), cdft_ref[...])
    seq_rows = z_ref.shape[1]
    for s in range(tm // seq_rows):
        rows, cols = slice(s * seq_rows, (s + 1) * seq_rows), slice(s * W_FOURIER, (s + 1) * W_FOURIER)
        z_ref[0, :, cols] = z[rows, :W_FOURIER].astype(BF16)
        z_ref[1, :, cols] = z[rows, W_FOURIER:].astype(BF16)

    lx_ref[:, :half] = f_lx[:, half:]
    lx_ref[:, half:] = lx_lg[:, :half]
    gl_ref[:, :half] = jax.nn.gelu(lx_lg[:, half:]).astype(BF16)
    gl_ref[:, half:] = jax.nn.gelu(lg_sb[:, :half]).astype(BF16)

    prod = sc_sx[:, :half] * sc_sx[:, half:]
    pos = lax.broadcasted_iota(jnp.int32, (tm, 1), 0) % row_len
    left = jnp.where(pos == 0, 0.0, pltpu.roll(prod, 1, axis=0))
    right = jnp.where(pos == row_len - 1, 0.0, pltpu.roll(prod, tm - 1, axis=0))
    conv = scw_ref[0:1, :] * left + scw_ref[1:2, :] * prod + scw_ref[2:3, :] * right
    yc_ref[...] = (lg_sb[:, half:] * conv).astype(BF16)

    chunk = 2 * half
    for j in range(gate_ref.shape[1] // chunk):
        half_g = proj(OFF_G + j * chunk)
        gate_ref[:, j * chunk:(j + 1) * chunk] = (0.5 + 0.5 * jnp.tanh(half_g)).astype(BF16)


def _inproj(h, mod, gain, w_in, cdft, sc_w, *, layer, mod_row, tm, row_len, seq_len=None):
    nb, t, d = h.shape
    n_cols = w_in.shape[-1]
    n_gate = n_cols - OFF_G
    seq_len = t if seq_len is None else seq_len
    assert tm % row_len == 0 and t % tm == 0 and (tm <= seq_len or (nb == 1 and tm % seq_len == 0))
    row = (lambda b: b) if mod_row is None else (lambda b: mod_row)
    tok = lambda w: pl.BlockSpec((None, tm, w), lambda b, i: (b, i, 0))
    if tm <= seq_len:
        z_spec = pl.BlockSpec((2, tm, W_FOURIER), lambda b, i: (0, i, b))
    else:
        z_spec = pl.BlockSpec((2, seq_len, tm // seq_len * W_FOURIER), lambda b, i: (0, 0, i))
    return pl.pallas_call(
        functools.partial(_inproj_kernel, row_len=row_len),
        out_shape=(
            jax.ShapeDtypeStruct((2, seq_len, nb * t // seq_len * W_FOURIER), BF16),
            jax.ShapeDtypeStruct((nb, t, W_LRU), F32),
            jax.ShapeDtypeStruct((nb, t, W_LRU), BF16),
            jax.ShapeDtypeStruct((nb, t, W_SC), BF16),
            jax.ShapeDtypeStruct((nb, t, n_gate), BF16),
        ),
        grid=(nb, t // tm),
        in_specs=[
            tok(d),
            pl.BlockSpec((None, None, N_MOD, d), lambda b, i: (layer, row(b), 0, 0)),
            _resident((None, 1, d), lambda b, i: (layer, 0, 0)),
            _resident((None, d, n_cols), lambda b, i: (layer, 0, 0)),
            _resident(cdft.shape, lambda b, i: (0, 0)),
            _resident((None, SC_CONV, W_SC), lambda b, i: (layer, 0, 0)),
        ],
        out_specs=(z_spec, tok(W_LRU), tok(W_LRU), tok(W_SC), tok(n_gate)),
        compiler_params=_params(("arbitrary", "arbitrary")),
        name="inproj",
    )(h, mod, gain, w_in, cdft, sc_w)


def _scan_layout(t):
    n_seg = 32 if t >= 1024 else 8
    seg_len = -(-t // n_seg)
    while seg_len % 8 != 4:
        seg_len += 1
    return n_seg, seg_len


def _lru_sequence(lx_ref, gl_ref, y_ref, h0, cw_ref, cb_ref, w_ref, b_ref, lam_ref,
                  xpad, a_f, b_f, a_b, b_b, h_f, h_b, hin_f, hin_b):
    t, cw = lx_ref.shape
    n_seg, seg_len = _scan_layout(t)
    chunk = min(t, LRU_CHUNK)
    t_pad = n_seg * seg_len
    halo = SUBLANES

    xpad[0:halo, :] = jnp.zeros((halo, cw), F32)
    xpad[halo + t:2 * halo + t, :] = jnp.zeros((halo, cw), F32)
    xpad[halo:halo + t, :] = lx_ref[...]
    if t_pad > t:
        for ref, val in ((a_f, 1.0), (a_b, 1.0), (b_f, 0.0), (b_b, 0.0)):
            ref[t:t_pad, :] = jnp.full((t_pad - t, cw), val, F32)

    neg_lam = -lam_ref[...]
    softplus = jnp.maximum(neg_lam, 0.0) + jnp.log1p(jnp.exp(-jnp.abs(neg_lam)))
    half_decay = (-0.5 * LRU_C) * softplus

    for k in range(t // chunk):
        first = k * chunk + halo - LRU_CONV // 2
        half_x = cb_ref[...]
        for j in range(LRU_CONV):
            half_x = half_x + cw_ref[j:j + 1, :] * xpad[first + j:first + j + chunk, :]
        half_gates = _dot(half_x.astype(BF16), w_ref[...]) + b_ref[...]
        rows = slice(k * chunk, (k + 1) * chunk)
        for d, (a_ref, bt_ref) in enumerate(((a_f, b_f), (a_b, b_b))):
            tanh_r = jnp.tanh(half_gates[:, (2 * d) * cw:(2 * d + 1) * cw])
            tanh_i = jnp.tanh(half_gates[:, (2 * d + 1) * cw:(2 * d + 2) * cw])
            log_a = half_decay[d:d + 1, :] + half_decay[d:d + 1, :] * tanh_r
            a = jnp.exp(log_a)
            one_minus_a2 = jnp.tanh(log_a) * (-1.0 - a * a)
            root = one_minus_a2 * lax.rsqrt(jnp.maximum(one_minus_a2, F32_TINY))
            a_ref[rows, :] = a
            bt_ref[rows, :] = root * (half_x + half_x * tanh_i)

    def seg(ref, j):
        return ref[pl.ds(j, n_seg, stride=seg_len), :]

    def summarize(j, carry):
        e_f, p_f, e_b, p_b = carry
        jb = seg_len - 1 - j
        af, ab = seg(a_f, j), seg(a_b, jb)
        return af * e_f + seg(b_f, j), p_f * af, ab * e_b + seg(b_b, jb), p_b * ab

    zeros = jnp.zeros((n_seg, cw), F32)
    ones = jnp.ones((n_seg, cw), F32)
    e_f, p_f, e_b, p_b = lax.fori_loop(0, seg_len, summarize, (zeros, ones, zeros, ones), unroll=SCAN_UNROLL)

    last_f, last_b = h0
    for s in range(n_seg):
        hin_f[s:s + 1, :] = last_f
        last_f = p_f[s:s + 1, :] * last_f + e_f[s:s + 1, :]
    for s in reversed(range(n_seg)):
        hin_b[s:s + 1, :] = last_b
        last_b = p_b[s:s + 1, :] * last_b + e_b[s:s + 1, :]

    def emit(j, carry):
        s_f, s_b = carry
        jb = seg_len - 1 - j
        s_f = seg(a_f, j) * s_f + seg(b_f, j)
        s_b = seg(a_b, jb) * s_b + seg(b_b, jb)
        h_f[pl.ds(j, n_seg, stride=seg_len), :] = s_f
        h_b[pl.ds(jb, n_seg, stride=seg_len), :] = s_b
        return s_f, s_b

    lax.fori_loop(0, seg_len, emit, (hin_f[0:n_seg, :], hin_b[0:n_seg, :]), unroll=SCAN_UNROLL)
    y_ref[...] = ((h_f[0:t, :] + h_b[0:t, :]) * gl_ref[...].astype(F32)).astype(BF16)
    return last_f, last_b


def _lru_kernel(lxc_ref, glc_ref, lx_ref, gl_ref, *rest):
    *param_refs, yc_ref, y_ref = rest[:7]
    scratch = rest[7:]
    zero = jnp.zeros((1, lx_ref.shape[1]), F32)
    ctx_state = _lru_sequence(lxc_ref, glc_ref, yc_ref, (zero, zero), *param_refs, *scratch)
    _lru_sequence(lx_ref, gl_ref, y_ref, ctx_state, *param_refs, *scratch)


def _lru(lxc, glc, lx, gl, conv_w, conv_b, w_cat, b_cat, lam, *, layer):
    nb, t, w = lx.shape
    tc = lxc.shape[1]
    cw = LRU_CW
    n_seg, seg_len = _scan_layout(t)
    t_pad = n_seg * seg_len
    seq = lambda rows: pl.BlockSpec((None, rows, cw), lambda b, c: (b, 0, c))
    scan_buf = pltpu.VMEM((t_pad, cw), F32)
    return pl.pallas_call(
        _lru_kernel,
        out_shape=(jax.ShapeDtypeStruct((nb, tc, w), BF16), jax.ShapeDtypeStruct((nb, t, w), BF16)),
        grid=(nb, w // cw),
        in_specs=[
            seq(tc), seq(tc), seq(t), seq(t),
            pl.BlockSpec((None, LRU_CONV, cw), lambda b, c: (layer, 0, c)),
            pl.BlockSpec((None, 1, cw), lambda b, c: (layer, 0, c)),
            pl.BlockSpec((None, None, cw, 4 * cw), lambda b, c: (layer, c, 0, 0)),
            pl.BlockSpec((None, None, 1, 4 * cw), lambda b, c: (layer, c, 0, 0)),
            pl.BlockSpec((None, 2, cw), lambda b, c: (layer, 0, c)),
        ],
        out_specs=(seq(tc), seq(t)),
        scratch_shapes=[pltpu.VMEM((t + 2 * SUBLANES, cw), F32)] + [scan_buf] * 6
        + [pltpu.VMEM((n_seg, cw), F32)] * 2,
        compiler_params=_params(("arbitrary", "arbitrary")),
        name="lru",
    )(lxc, glc, lx, gl, conv_w, conv_b, w_cat, b_cat, lam)


def _lru_gate_weights(wa, ba, wx, bx):
    depth, _, heads, blk, _ = wa.shape
    per = LRU_CW // blk
    groups = heads // per
    eye = jnp.eye(per, dtype=wa.dtype)

    def block_diag(w):
        w = w.reshape(depth, 2, groups, per, blk, blk)
        return jnp.einsum("ldgaij,ab->ldgaibj", w, eye).reshape(depth, 2, groups, LRU_CW, LRU_CW)

    w = jnp.stack([block_diag(wa), block_diag(wx)], axis=2)
    w = w.transpose(0, 3, 4, 1, 2, 5).reshape(depth, groups, LRU_CW, 4 * LRU_CW)
    b = jnp.stack([ba.reshape(depth, 2, groups, LRU_CW), bx.reshape(depth, 2, groups, LRU_CW)], axis=2)
    b = b.transpose(0, 3, 1, 2, 4).reshape(depth, groups, 1, 4 * LRU_CW)
    return w.astype(BF16), 0.5 * b


DFT_EXTRA_ROWS = 16


def _seqdft_kernel(d_ref, dnext_ref, z_ref, lo_ref, hi_ref):
    tm = d_ref.shape[0]
    t = z_ref.shape[0] // 2
    a = _dot(jnp.concatenate([d_ref[:, :t], dnext_ref[:, :t]], axis=0), z_ref[:t, :])
    b = _dot(jnp.concatenate([d_ref[:, t:], dnext_ref[:, t:]], axis=0), z_ref[t:, :])
    lo_ref[...] = (a[:tm] + b[:tm]).astype(BF16)
    diff = (a - b).astype(BF16)
    row = lax.broadcasted_iota(jnp.int32, (tm, tm + DFT_EXTRA_ROWS), 0)
    col = lax.broadcasted_iota(jnp.int32, (tm, tm + DFT_EXTRA_ROWS), 1)
    flip = jnp.where(col == tm - row, 1.0, 0.0).astype(BF16)
    hi_ref[...] = _dot(flip, diff).astype(BF16)


def _seqdft(dmat, z):
    t = z.shape[1]
    n = z.shape[-1]
    half = t // 2
    tm = min(half, MXU_DIM)
    n_tiles = half // tm
    assert dmat.shape == (half + DFT_EXTRA_ROWS, 2 * t)
    return pl.pallas_call(
        _seqdft_kernel,
        out_shape=(jax.ShapeDtypeStruct((half, n), BF16),) * 2,
        grid=(n_tiles,),
        in_specs=[
            pl.BlockSpec((tm, 2 * t), lambda i: (i, 0)),
            pl.BlockSpec((DFT_EXTRA_ROWS, 2 * t), lambda i: ((i + 1) * (tm // DFT_EXTRA_ROWS), 0)),
            _resident((2 * t, n), lambda i: (0, 0)),
        ],
        out_specs=(pl.BlockSpec((tm, n), lambda i: (i, 0)),
                   pl.BlockSpec((tm, n), lambda i: (n_tiles - 1 - i, 0))),
        compiler_params=_params(("arbitrary",)),
        name="seqdft",
    )(dmat, dmat, z.reshape(2 * t, n))


def _channel_dft():
    gw = W_FOURIER // FOURIER_GROUPS
    idx = np.arange(gw)
    ang = 2.0 * np.pi * ((idx[:, None] * idx[None, :]) % gw) / gw
    eye = np.eye(FOURIER_GROUPS)
    cos = np.kron(eye, np.cos(ang)) / np.sqrt(gw)
    sin = np.kron(eye, np.sin(ang)) / np.sqrt(gw)
    return jnp.asarray(np.concatenate([cos, -sin], axis=1), dtype=F32).astype(BF16)


def _sequence_dft(t):
    rows = t // 2 + DFT_EXTRA_ROWS

    def table(n_cols, col_step):
        j = lax.broadcasted_iota(jnp.int32, (rows, n_cols), 0)
        k = lax.broadcasted_iota(jnp.int32, (rows, n_cols), 1) * col_step
        ang = ((j * k) % t).astype(F32) * (2.0 * np.pi / t)
        return jnp.cos(ang), jnp.sin(ang)

    q = 64 if t % 64 == 0 and t > 64 else 1
    c1, s1 = table(t // q, q)
    c2, s2 = table(q, 1)
    cos = c1[:, :, None] * c2[:, None, :] - s1[:, :, None] * s2[:, None, :]
    sin = s1[:, :, None] * c2[:, None, :] + c1[:, :, None] * s2[:, None, :]
    both = jnp.concatenate([cos.reshape(rows, t), sin.reshape(rows, t)], axis=1)
    return (both * (1.0 / np.sqrt(t))).astype(BF16)


def kernel(x, c, ctx, c_ctx, w_ada, b_ada, g_ffn1, g_mix, g_ffn2, ffn_w_gate, ffn_w_up, ffn_w_down, w_in, lru_conv_w, lru_conv_b, lru_wa, lru_ba, lru_wx, lru_bx, lru_lambda, sc_conv_w, wp_fourier, wp_lru, wp_conv, w_out, g_final):
    depth, d = g_ffn1.shape
    batch, seq, _ = x.shape
    ctx_len = ctx.shape[1]
    ctx_row = batch

    mod = _modulation(c, c_ctx, w_ada, b_ada)
    wg, wu, wd = (0.5 * ffn_w_gate).astype(BF16), ffn_w_up.astype(BF16), ffn_w_down.astype(BF16)
    col_scale = jnp.where(jnp.arange(w_in.shape[-1]) >= OFF_G, 0.5, 1.0).astype(F32)
    w_in_b = (w_in * col_scale).astype(BF16)
    branch_w = tuple(w.astype(BF16) for w in (wp_fourier, wp_lru, wp_conv, w_out))
    gate_w, gate_b = _lru_gate_weights(lru_wa, lru_ba, lru_wx, lru_bx)
    gains = [g.reshape(depth, 1, d) for g in (g_ffn1, g_mix, g_ffn2)]
    g_fin = g_final.reshape(1, d)
    half_conv_w, half_conv_b = 0.5 * lru_conv_w, 0.5 * lru_conv_b.reshape(depth, 1, W_LRU)
    cdft = _channel_dft()
    dft_lat, dft_ctx = _sequence_dft(seq), _sequence_dft(ctx_len)

    lat_ffn = dict(mod_row=None, tm=1024)
    lat_merge = dict(mod_row=None, tm=512)
    lat_proj = dict(mod_row=None, tm=1024)
    cx_flat = dict(mod_row=ctx_row, tm=min(1024, batch * ctx_len))
    cx_merge = dict(mod_row=ctx_row, tm=min(512, batch * ctx_len))
    flat = lambda a: a.reshape(1, batch * ctx_len, a.shape[-1])
    per_seq = lambda a: a.reshape(batch, ctx_len, a.shape[-1])

    h, hc = x, flat(ctx)
    for l in range(depth):
        last = l == depth - 1
        ffn = functools.partial(_ffn, mod=mod, g_final=g_fin, wg=wg, wu=wu, wd=wd, layer=l)
        inproj = functools.partial(_inproj, mod=mod, gain=gains[1], w_in=w_in_b, cdft=cdft, sc_w=sc_conv_w, layer=l)
        lru = functools.partial(_lru, conv_w=half_conv_w, conv_b=half_conv_b, w_cat=gate_w, b_cat=gate_b,
                                lam=lru_lambda, layer=l)

        hc = ffn(hc, gain=gains[0], half=0, base=0, **cx_flat)
        h = ffn(h, gain=gains[0], half=0, base=0, **lat_ffn)

        zc, lxc, glc, ycc, gatec = inproj(hc, row_len=ctx_len, seq_len=ctx_len, **cx_flat)
        z, lx, gl, yc, gate = inproj(h, row_len=GRID_W, **lat_proj)
        ylc, yl = lru(per_seq(lxc), per_seq(glc), lx, gl)
        h = ffn(h, gain=gains[2], half=1, base=6, final=last,
                merge=(*_seqdft(dft_lat, z), yl, yc, gate, *branch_w), **lat_merge)
        if not last:
            hc = ffn(hc, gain=gains[2], half=1, base=6, seq_len=ctx_len,
                     merge=(*_seqdft(dft_ctx, zc), flat(ylc), ycc, gatec, *branch_w), **cx_merge)
    return h
```

```python
import functools

import numpy as np
import jax
import jax.numpy as jnp
from jax import lax
from jax.experimental import pallas as pl
from jax.experimental.pallas import tpu as pltpu

GRID_W = 64
N_MOD = 9
W_FOURIER = 256
FOURIER_GROUPS = 4
W_LRU = 512
LRU_HEADS = 8
W_SC = 256
LRU_CONV = 4
SC_CONV = 3
LRU_C = 8.0
N_BRANCH = 3
RMS_EPS = 1e-6
OFF_F = 0
OFF_LX = OFF_F + W_FOURIER
OFF_LG = OFF_LX + W_LRU
OFF_SB = OFF_LG + W_LRU
OFF_SC = OFF_SB + W_SC
OFF_SX = OFF_SC + W_SC
OFF_G = OFF_SX + W_SC

LANES = 128
SUBLANES = 8
MXU_DIM = 256
VMEM_LIMIT = 56 * 1024 * 1024
MOD_ROWS = 16
LRU_CW = LANES
BF16 = jnp.bfloat16
F32 = jnp.float32
F32_TINY = float(np.finfo(np.float32).tiny)
SCAN_UNROLL = 4
LRU_CHUNK = 512


def _params(sem):
    return pltpu.CompilerParams(dimension_semantics=sem, vmem_limit_bytes=VMEM_LIMIT)


def _resident(block_shape, index_map):
    return pl.BlockSpec(block_shape, index_map, pipeline_mode=pl.Buffered(1))


def _dot(a, b):
    return jnp.dot(a, b, preferred_element_type=F32)


def _rms_mod(x, gain, shift, scale):
    y = x * lax.rsqrt(jnp.mean(x * x, axis=-1, keepdims=True) + RMS_EPS)
    return y * (gain * (1.0 + scale)) + shift


def _mod_kernel(s_ref, w_ref, b_ref, o_ref):
    s = jax.nn.silu(s_ref[...]).astype(BF16)
    part = _dot(s, w_ref[...].astype(BF16))

    @pl.when(pl.program_id(1) == 0)
    def _():
        o_ref[...] = part + b_ref[...]

    @pl.when(pl.program_id(1) > 0)
    def _():
        o_ref[...] += part


def _modulation(c, c_ctx, w_ada, b_ada):
    depth, d, n = w_ada.shape
    batch = c.shape[0]
    rows = jnp.zeros((MOD_ROWS, d), F32).at[:batch].set(c).at[batch].set(c_ctx)
    tk = MXU_DIM
    out = pl.pallas_call(
        _mod_kernel,
        out_shape=jax.ShapeDtypeStruct((depth, MOD_ROWS, n), F32),
        grid=(depth, d // tk),
        in_specs=[
            pl.BlockSpec((MOD_ROWS, tk), lambda l, k: (0, k)),
            pl.BlockSpec((None, tk, n), lambda l, k: (l, k, 0)),
            pl.BlockSpec((None, 1, n), lambda l, k: (l, 0, 0)),
        ],
        out_specs=pl.BlockSpec((None, MOD_ROWS, n), lambda l, k: (l, 0, 0)),
        compiler_params=_params(("arbitrary", "arbitrary")),
        name="modulation",
    )(rows, w_ada, b_ada.reshape(depth, 1, n))
    return out.reshape(depth, MOD_ROWS, N_MOD, d)


def _swiglu_step(x, mod_ref, g_ref, gfin_ref, wg_ref, wu_ref, wd_ref, acc_ref, *, base, chunk, final):
    u = _rms_mod(x, g_ref[...], mod_ref[base:base + 1, :], mod_ref[base + 1:base + 2, :]).astype(BF16)
    d_ff = wg_ref.shape[1]
    assert d_ff // chunk >= 2
    for j in range(d_ff // chunk):
        sl = slice(j * chunk, (j + 1) * chunk)
        half_g = _dot(u, wg_ref[:, sl])
        a = ((half_g + half_g * jnp.tanh(half_g)) * _dot(u, wu_ref[:, sl])).astype(BF16)
        part = _dot(a, wd_ref[sl, :])
        if j == 0:
            acc_ref[...] = part
        elif j < d_ff // chunk - 1:
            acc_ref[...] += part
    y = x + (0.5 * mod_ref[base + 2:base + 3, :]) * (acc_ref[...] + part)
    if final:
        y = (y * lax.rsqrt(jnp.mean(y * y, axis=-1, keepdims=True) + RMS_EPS)) * gfin_ref[...]
    return y


def _ffn_kernel(x_ref, mod_ref, *rest, base, final):
    *ffn_refs, o_ref, acc_ref = rest
    o_ref[...] = _swiglu_step(x_ref[...], mod_ref, *ffn_refs, acc_ref, base=base, chunk=MXU_DIM, final=final)


def _mergeffn_kernel(h_ref, mod_ref, yf_lo_ref, yf_hi_ref, yl_ref, yc_ref, gate_ref,
                     wpf_ref, wpl_ref, wpc_ref, wo_ref, *rest, base, final):
    *ffn_refs, o_ref, acc_ref = rest
    tm, d = h_ref.shape
    if yf_lo_ref.shape[0] == tm:
        in_lo = pl.program_id(1) < pl.num_programs(1) // 2
        yf = jnp.where(in_lo, yf_lo_ref[...], yf_hi_ref[...])
    else:
        pieces = []
        for s in range(yf_lo_ref.shape[1] // W_FOURIER):
            cols = slice(s * W_FOURIER, (s + 1) * W_FOURIER)
            pieces += [yf_lo_ref[:, cols], yf_hi_ref[:, cols]]
        yf = jnp.concatenate(pieces, axis=0)
    m = gate_ref[:, 0:d].astype(F32) * _dot(yf, wpf_ref[...])
    m = m + gate_ref[:, d:2 * d].astype(F32) * _dot(yl_ref[...], wpl_ref[...])
    m = m + gate_ref[:, 2 * d:3 * d].astype(F32) * _dot(yc_ref[...], wpc_ref[...])
    h = h_ref[...] + mod_ref[5:6, :] * _dot(m.astype(BF16), wo_ref[...])
    o_ref[...] = _swiglu_step(h, mod_ref, *ffn_refs, acc_ref, base=base, chunk=MXU_DIM, final=final)


def _ffn(h, mod, gain, g_final, wg, wu, wd, *, layer, half, mod_row, base, tm, final=False, merge=None,
         seq_len=None):
    nb, t, d = h.shape
    d_ff = wg.shape[-1]
    row = (lambda b: b) if mod_row is None else (lambda b: mod_row)
    tok = lambda w: pl.BlockSpec((None, tm, w), lambda b, i: (b, i, 0))
    merge_specs, merge_args, body = [], (), _ffn_kernel
    if merge is not None:
        wspec = lambda w: _resident((None, w, d), lambda b, i: (layer, 0, 0))
        seq_len = t if seq_len is None else seq_len
        if tm < seq_len:
            assert t == seq_len
            half_tiles = seq_len // 2 // tm
            yf_lo = pl.BlockSpec((tm, W_FOURIER), lambda b, i: (jnp.minimum(i, half_tiles - 1), b))
            yf_hi = pl.BlockSpec((tm, W_FOURIER), lambda b, i: (jnp.maximum(i - half_tiles, 0), b))
        else:
            per_tile, per_row = tm // seq_len, t // seq_len
            yf_lo = yf_hi = pl.BlockSpec((seq_len // 2, per_tile * W_FOURIER),
                                         lambda b, i: (0, (b * per_row) // per_tile + i))
        merge_specs = [yf_lo, yf_hi,
                       tok(W_LRU), tok(W_SC), tok(N_BRANCH * d),
                       wspec(W_FOURIER), wspec(W_LRU), wspec(W_SC), wspec(d)]
        merge_args, body = tuple(merge), _mergeffn_kernel
    return pl.pallas_call(
        functools.partial(body, base=base, final=final),
        out_shape=jax.ShapeDtypeStruct(h.shape, F32),
        grid=(nb, t // tm),
        in_specs=[
            tok(d),
            pl.BlockSpec((None, None, N_MOD, d), lambda b, i: (layer, row(b), 0, 0)),
            *merge_specs,
            _resident((None, 1, d), lambda b, i: (layer, 0, 0)),
            _resident((1, d), lambda b, i: (0, 0)),
            _resident((None, None, d, d_ff), lambda b, i: (layer, half, 0, 0)),
            _resident((None, None, d, d_ff), lambda b, i: (layer, half, 0, 0)),
            _resident((None, None, d_ff, d), lambda b, i: (layer, half, 0, 0)),
        ],
        out_specs=tok(d),
        scratch_shapes=[pltpu.VMEM((tm, d), F32)],
        compiler_params=_params(("arbitrary", "arbitrary")),
        name="ffn" if merge is None else "mergeffn",
    )(h, mod, *merge_args, gain, g_final, wg, wu, wd)


def _inproj_kernel(x_ref, mod_ref, g_ref, w_ref, cdft_ref, scw_ref,
                   z_ref, lx_ref, gl_ref, yc_ref, gate_ref, *, row_len):
    tm = x_ref.shape[0]
    u = _rms_mod(x_ref[...], g_ref[...], mod_ref[3:4, :], mod_ref[4:5, :]).astype(BF16)

    half = MXU_DIM
    assert (OFF_LX, OFF_LG, OFF_SB, OFF_SC, OFF_G) == (half, 3 * half, 5 * half, 6 * half, 8 * half)

    def proj(start):
        return _dot(u, w_ref[:, start:start + 2 * half])

    f_lx = proj(OFF_F)
    lx_lg = proj(OFF_LX + half)
    lg_sb = proj(OFF_LG + half)
    sc_sx = proj(OFF_SC)

    z = _dot(f_lx[:, :half].astype(BF16), cdft_ref[...])
    seq_rows = z_ref.shape[1]
    for s in range(tm // seq_rows):
        rows, cols = slice(s * seq_rows, (s + 1) * seq_rows), slice(s * W_FOURIER, (s + 1) * W_FOURIER)
        z_ref[0, :, cols] = z[rows, :W_FOURIER].astype(BF16)
        z_ref[1, :, cols] = z[rows, W_FOURIER:].astype(BF16)

    lx_ref[:, :half] = f_lx[:, half:]
    lx_ref[:, half:] = lx_lg[:, :half]
    gl_ref[:, :half] = jax.nn.gelu(lx_lg[:, half:]).astype(BF16)
    gl_ref[:, half:] = jax.nn.gelu(lg_sb[:, :half]).astype(BF16)

    prod = sc_sx[:, :half] * sc_sx[:, half:]
    pos = lax.broadcasted_iota(jnp.int32, (tm, 1), 0) % row_len
    left = jnp.where(pos == 0, 0.0, pltpu.roll(prod, 1, axis=0))
    right = jnp.where(pos == row_len - 1, 0.0, pltpu.roll(prod, tm - 1, axis=0))
    conv = scw_ref[0:1, :] * left + scw_ref[1:2, :] * prod + scw_ref[2:3, :] * right
    yc_ref[...] = (lg_sb[:, half:] * conv).astype(BF16)

    chunk = 2 * half
    for j in range(gate_ref.shape[1] // chunk):
        half_g = proj(OFF_G + j * chunk)
        gate_ref[:, j * chunk:(j + 1) * chunk] = (0.5 + 0.5 * jnp.tanh(half_g)).astype(BF16)


def _inproj(h, mod, gain, w_in, cdft, sc_w, *, layer, mod_row, tm, row_len, seq_len=None):
    nb, t, d = h.shape
    n_cols = w_in.shape[-1]
    n_gate = n_cols - OFF_G
    seq_len = t if seq_len is None else seq_len
    assert tm % row_len == 0 and t % tm == 0 and (tm <= seq_len or (nb == 1 and tm % seq_len == 0))
    row = (lambda b: b) if mod_row is None else (lambda b: mod_row)
    tok = lambda w: pl.BlockSpec((None, tm, w), lambda b, i: (b, i, 0))
    if tm <= seq_len:
        z_spec = pl.BlockSpec((2, tm, W_FOURIER), lambda b, i: (0, i, b))
    else:
        z_spec = pl.BlockSpec((2, seq_len, tm // seq_len * W_FOURIER), lambda b, i: (0, 0, i))
    return pl.pallas_call(
        functools.partial(_inproj_kernel, row_len=row_len),
        out_shape=(
            jax.ShapeDtypeStruct((2, seq_len, nb * t // seq_len * W_FOURIER), BF16),
            jax.ShapeDtypeStruct((nb, t, W_LRU), F32),
            jax.ShapeDtypeStruct((nb, t, W_LRU), BF16),
            jax.ShapeDtypeStruct((nb, t, W_SC), BF16),
            jax.ShapeDtypeStruct((nb, t, n_gate), BF16),
        ),
        grid=(nb, t // tm),
        in_specs=[
            tok(d),
            pl.BlockSpec((None, None, N_MOD, d), lambda b, i: (layer, row(b), 0, 0)),
            _resident((None, 1, d), lambda b, i: (layer, 0, 0)),
            _resident((None, d, n_cols), lambda b, i: (layer, 0, 0)),
            _resident(cdft.shape, lambda b, i: (0, 0)),
            _resident((None, SC_CONV, W_SC), lambda b, i: (layer, 0, 0)),
        ],
        out_specs=(z_spec, tok(W_LRU), tok(W_LRU), tok(W_SC), tok(n_gate)),
        compiler_params=_params(("arbitrary", "arbitrary")),
        name="inproj",
    )(h, mod, gain, w_in, cdft, sc_w)


def _scan_layout(t):
    n_seg = 32 if t >= 1024 else 8
    seg_len = -(-t // n_seg)
    while seg_len % 8 != 4:
        seg_len += 1
    return n_seg, seg_len


def _lru_sequence(lx_ref, gl_ref, y_ref, h0, cw_ref, cb_ref, w_ref, b_ref, lam_ref,
                  xpad, a_f, b_f, a_b, b_b, h_f, h_b, hin_f, hin_b):
    t, cw = lx_ref.shape
    n_seg, seg_len = _scan_layout(t)
    chunk = min(t, LRU_CHUNK)
    t_pad = n_seg * seg_len
    halo = SUBLANES

    xpad[0:halo, :] = jnp.zeros((halo, cw), F32)
    xpad[halo + t:2 * halo + t, :] = jnp.zeros((halo, cw), F32)
    xpad[halo:halo + t, :] = lx_ref[...]
    if t_pad > t:
        for ref, val in ((a_f, 1.0), (a_b, 1.0), (b_f, 0.0), (b_b, 0.0)):
            ref[t:t_pad, :] = jnp.full((t_pad - t, cw), val, F32)

    neg_lam = -lam_ref[...]
    softplus = jnp.maximum(neg_lam, 0.0) + jnp.log1p(jnp.exp(-jnp.abs(neg_lam)))
    half_decay = (-0.5 * LRU_C) * softplus

    for k in range(t // chunk):
        first = k * chunk + halo - LRU_CONV // 2
        half_x = cb_ref[...]
        for j in range(LRU_CONV):
            half_x = half_x + cw_ref[j:j + 1, :] * xpad[first + j:first + j + chunk, :]
        half_gates = _dot(half_x.astype(BF16), w_ref[...]) + b_ref[...]
        rows = slice(k * chunk, (k + 1) * chunk)
        for d, (a_ref, bt_ref) in enumerate(((a_f, b_f), (a_b, b_b))):
            tanh_r = jnp.tanh(half_gates[:, (2 * d) * cw:(2 * d + 1) * cw])
            tanh_i = jnp.tanh(half_gates[:, (2 * d + 1) * cw:(2 * d + 2) * cw])
            log_a = half_decay[d:d + 1, :] + half_decay[d:d + 1, :] * tanh_r
            a = jnp.exp(log_a)
            one_minus_a2 = jnp.tanh(log_a) * (-1.0 - a * a)
            root = one_minus_a2 * lax.rsqrt(jnp.maximum(one_minus_a2, F32_TINY))
            a_ref[rows, :] = a
            bt_ref[rows, :] = root * (half_x + half_x * tanh_i)

    def seg(ref, j):
        return ref[pl.ds(j, n_seg, stride=seg_len), :]

    def summarize(j, carry):
        e_f, p_f, e_b, p_b = carry
        jb = seg_len - 1 - j
        af, ab = seg(a_f, j), seg(a_b, jb)
        return af * e_f + seg(b_f, j), p_f * af, ab * e_b + seg(b_b, jb), p_b * ab

    zeros = jnp.zeros((n_seg, cw), F32)
    ones = jnp.ones((n_seg, cw), F32)
    e_f, p_f, e_b, p_b = lax.fori_loop(0, seg_len, summarize, (zeros, ones, zeros, ones), unroll=SCAN_UNROLL)

    last_f, last_b = h0
    for s in range(n_seg):
        hin_f[s:s + 1, :] = last_f
        last_f = p_f[s:s + 1, :] * last_f + e_f[s:s + 1, :]
    for s in reversed(range(n_seg)):
        hin_b[s:s + 1, :] = last_b
        last_b = p_b[s:s + 1, :] * last_b + e_b[s:s + 1, :]

    def emit(j, carry):
        s_f, s_b = carry
        jb = seg_len - 1 - j
        s_f = seg(a_f, j) * s_f + seg(b_f, j)
        s_b = seg(a_b, jb) * s_b + seg(b_b, jb)
        h_f[pl.ds(j, n_seg, stride=seg_len), :] = s_f
        h_b[pl.ds(jb, n_seg, stride=seg_len), :] = s_b
        return s_f, s_b

    lax.fori_loop(0, seg_len, emit, (hin_f[0:n_seg, :], hin_b[0:n_seg, :]), unroll=SCAN_UNROLL)
    y_ref[...] = ((h_f[0:t, :] + h_b[0:t, :]) * gl_ref[...].astype(F32)).astype(BF16)
    return last_f, last_b


def _lru_kernel(lxc_ref, glc_ref, lx_ref, gl_ref, *rest):
    *param_refs, yc_ref, y_ref = rest[:7]
    scratch = rest[7:]
    zero = jnp.zeros((1, lx_ref.shape[1]), F32)
    ctx_state = _lru_sequence(lxc_ref, glc_ref, yc_ref, (zero, zero), *param_refs, *scratch)
    _lru_sequence(lx_ref, gl_ref, y_ref, ctx_state, *param_refs, *scratch)


def _lru(lxc, glc, lx, gl, conv_w, conv_b, w_cat, b_cat, lam, *, layer):
    nb, t, w = lx.shape
    tc = lxc.shape[1]
    cw = LRU_CW
    n_seg, seg_len = _scan_layout(t)
    t_pad = n_seg * seg_len
    seq = lambda rows: pl.BlockSpec((None, rows, cw), lambda b, c: (b, 0, c))
    scan_buf = pltpu.VMEM((t_pad, cw), F32)
    return pl.pallas_call(
        _lru_kernel,
        out_shape=(jax.ShapeDtypeStruct((nb, tc, w), BF16), jax.ShapeDtypeStruct((nb, t, w), BF16)),
        grid=(nb, w // cw),
        in_specs=[
            seq(tc), seq(tc), seq(t), seq(t),
            pl.BlockSpec((None, LRU_CONV, cw), lambda b, c: (layer, 0, c)),
            pl.BlockSpec((None, 1, cw), lambda b, c: (layer, 0, c)),
            pl.BlockSpec((None, None, cw, 4 * cw), lambda b, c: (layer, c, 0, 0)),
            pl.BlockSpec((None, None, 1, 4 * cw), lambda b, c: (layer, c, 0, 0)),
            pl.BlockSpec((None, 2, cw), lambda b, c: (layer, 0, c)),
        ],
        out_specs=(seq(tc), seq(t)),
        scratch_shapes=[pltpu.VMEM((t + 2 * SUBLANES, cw), F32)] + [scan_buf] * 6
        + [pltpu.VMEM((n_seg, cw), F32)] * 2,
        compiler_params=_params(("arbitrary", "arbitrary")),
        name="lru",
    )(lxc, glc, lx, gl, conv_w, conv_b, w_cat, b_cat, lam)


def _lru_gate_weights(wa, ba, wx, bx):
    depth, _, heads, blk, _ = wa.shape
    per = LRU_CW // blk
    groups = heads // per
    eye = jnp.eye(per, dtype=wa.dtype)

    def block_diag(w):
        w = w.reshape(depth, 2, groups, per, blk, blk)
        return jnp.einsum("ldgaij,ab->ldgaibj", w, eye).reshape(depth, 2, groups, LRU_CW, LRU_CW)

    w = jnp.stack([block_diag(wa), block_diag(wx)], axis=2)
    w = w.transpose(0, 3, 4, 1, 2, 5).reshape(depth, groups, LRU_CW, 4 * LRU_CW)
    b = jnp.stack([ba.reshape(depth, 2, groups, LRU_CW), bx.reshape(depth, 2, groups, LRU_CW)], axis=2)
    b = b.transpose(0, 3, 1, 2, 4).reshape(depth, groups, 1, 4 * LRU_CW)
    return w.astype(BF16), 0.5 * b


DFT_EXTRA_ROWS = 16


def _seqdft_kernel(d_ref, dnext_ref, z_ref, lo_ref, hi_ref):
    tm = d_ref.shape[0]
    t = z_ref.shape[0] // 2
    a = _dot(jnp.concatenate([d_ref[:, :t], dnext_ref[:, :t]], axis=0), z_ref[:t, :])
    b = _dot(jnp.concatenate([d_ref[:, t:], dnext_ref[:, t:]], axis=0), z_ref[t:, :])
    lo_ref[...] = (a[:tm] + b[:tm]).astype(BF16)
    diff = (a - b).astype(BF16)
    row = lax.broadcasted_iota(jnp.int32, (tm, tm + DFT_EXTRA_ROWS), 0)
    col = lax.broadcasted_iota(jnp.int32, (tm, tm + DFT_EXTRA_ROWS), 1)
    flip = jnp.where(col == tm - row, 1.0, 0.0).astype(BF16)
    hi_ref[...] = _dot(flip, diff).astype(BF16)


def _seqdft(dmat, z):
    t = z.shape[1]
    n = z.shape[-1]
    half = t // 2
    tm = min(half, MXU_DIM)
    n_tiles = half // tm
    assert dmat.shape == (half + DFT_EXTRA_ROWS, 2 * t)
    return pl.pallas_call(
        _seqdft_kernel,
        out_shape=(jax.ShapeDtypeStruct((half, n), BF16),) * 2,
        grid=(n_tiles,),
        in_specs=[
            pl.BlockSpec((tm, 2 * t), lambda i: (i, 0)),
            pl.BlockSpec((DFT_EXTRA_ROWS, 2 * t), lambda i: ((i + 1) * (tm // DFT_EXTRA_ROWS), 0)),
            _resident((2 * t, n), lambda i: (0, 0)),
        ],
        out_specs=(pl.BlockSpec((tm, n), lambda i: (i, 0)),
                   pl.BlockSpec((tm, n), lambda i: (n_tiles - 1 - i, 0))),
        compiler_params=_params(("arbitrary",)),
        name="seqdft",
    )(dmat, dmat, z.reshape(2 * t, n))


def _channel_dft():
    gw = W_FOURIER // FOURIER_GROUPS
    idx = np.arange(gw)
    ang = 2.0 * np.pi * ((idx[:, None] * idx[None, :]) % gw) / gw
    eye = np.eye(FOURIER_GROUPS)
    cos = np.kron(eye, np.cos(ang)) / np.sqrt(gw)
    sin = np.kron(eye, np.sin(ang)) / np.sqrt(gw)
    return jnp.asarray(np.concatenate([cos, -sin], axis=1), dtype=F32).astype(BF16)


def _sequence_dft(t):
    rows = t // 2 + DFT_EXTRA_ROWS

    def table(n_cols, col_step):
        j = lax.broadcasted_iota(jnp.int32, (rows, n_cols), 0)
        k = lax.broadcasted_iota(jnp.int32, (rows, n_cols), 1) * col_step
        ang = ((j * k) % t).astype(F32) * (2.0 * np.pi / t)
        return jnp.cos(ang), jnp.sin(ang)

    q = 64 if t % 64 == 0 and t > 64 else 1
    c1, s1 = table(t // q, q)
    c2, s2 = table(q, 1)
    lead = jnp.stack([c1, s1], axis=1)[:, :, :, None]
    cross = jnp.stack([-s1, c1], axis=1)[:, :, :, None]
    both = lead * c2[:, None, None, :] + cross * s2[:, None, None, :]
    return (both.reshape(rows, 2 * t) * (1.0 / np.sqrt(t))).astype(BF16)


def kernel(x, c, ctx, c_ctx, w_ada, b_ada, g_ffn1, g_mix, g_ffn2, ffn_w_gate, ffn_w_up, ffn_w_down, w_in, lru_conv_w, lru_conv_b, lru_wa, lru_ba, lru_wx, lru_bx, lru_lambda, sc_conv_w, wp_fourier, wp_lru, wp_conv, w_out, g_final):
    depth, d = g_ffn1.shape
    batch, seq, _ = x.shape
    ctx_len = ctx.shape[1]
    ctx_row = batch

    mod = _modulation(c, c_ctx, w_ada, b_ada)
    wg, wu, wd = (0.5 * ffn_w_gate).astype(BF16), ffn_w_up.astype(BF16), ffn_w_down.astype(BF16)
    col_scale = jnp.where(jnp.arange(w_in.shape[-1]) >= OFF_G, 0.5, 1.0).astype(F32)
    w_in_b = (w_in * col_scale).astype(BF16)
    branch_w = tuple(w.astype(BF16) for w in (wp_fourier, wp_lru, wp_conv, w_out))
    gate_w, gate_b = _lru_gate_weights(lru_wa, lru_ba, lru_wx, lru_bx)
    gains = [g.reshape(depth, 1, d) for g in (g_ffn1, g_mix, g_ffn2)]
    g_fin = g_final.reshape(1, d)
    half_conv_w, half_conv_b = 0.5 * lru_conv_w, 0.5 * lru_conv_b.reshape(depth, 1, W_LRU)
    cdft = _channel_dft()
    dft_lat, dft_ctx = _sequence_dft(seq), _sequence_dft(ctx_len)

    lat_ffn = dict(mod_row=None, tm=1024)
    lat_merge = dict(mod_row=None, tm=512)
    lat_proj = dict(mod_row=None, tm=1024)
    cx_flat = dict(mod_row=ctx_row, tm=min(1024, batch * ctx_len))
    cx_merge = dict(mod_row=ctx_row, tm=min(512, batch * ctx_len))
    flat = lambda a: a.reshape(1, batch * ctx_len, a.shape[-1])
    per_seq = lambda a: a.reshape(batch, ctx_len, a.shape[-1])

    h, hc = x, flat(ctx)
    for l in range(depth):
        last = l == depth - 1
        ffn = functools.partial(_ffn, mod=mod, g_final=g_fin, wg=wg, wu=wu, wd=wd, layer=l)
        inproj = functools.partial(_inproj, mod=mod, gain=gains[1], w_in=w_in_b, cdft=cdft, sc_w=sc_conv_w, layer=l)
        lru = functools.partial(_lru, conv_w=half_conv_w, conv_b=half_conv_b, w_cat=gate_w, b_cat=gate_b,
                                lam=lru_lambda, layer=l)

        hc = ffn(hc, gain=gains[0], half=0, base=0, **cx_flat)
        h = ffn(h, gain=gains[0], half=0, base=0, **lat_ffn)

        zc, lxc, glc, ycc, gatec = inproj(hc, row_len=ctx_len, seq_len=ctx_len, **cx_flat)
        z, lx, gl, yc, gate = inproj(h, row_len=GRID_W, **lat_proj)
        ylc, yl = lru(per_seq(lxc), per_seq(glc), lx, gl)
        h = ffn(h, gain=gains[2], half=1, base=6, final=last,
                merge=(*_seqdft(dft_lat, z), yl, yc, gate, *branch_w), **lat_merge)
        if not last:
            hc = ffn(hc, gain=gains[2], half=1, base=6, seq_len=ctx_len,
                     merge=(*_seqdft(dft_ctx, zc), flat(ylc), ycc, gatec, *branch_w), **cx_merge)
    return h
```
